```python
import math
import jax
import jax.numpy as jnp
from jax import lax
import numpy as np

D_MODEL = 1024
BATCH = 16
SEQ = 4096
DEPTH = 2
DEC_BATCH = 8
DEC_SEQ = 64
PAST_LEN = 4096

CHUNK = 64
MIX_WIDTH = D_MODEL
GDN_HEADS = 4
GDN_HEAD_DIM = 128
GDN_WIDTH = GDN_HEADS * GDN_HEAD_DIM
QKV_WIDTH = 3 * GDN_WIDTH
CONV_W = 4
SMLP_GROUPS = 4
SMLP_WIDTH = MIX_WIDTH - GDN_WIDTH
SMLP_GROUP_DIM = SMLP_WIDTH // SMLP_GROUPS
SMLP_CHUNK = 128
D_FF = 4 * D_MODEL
IN_WIDTH = QKV_WIDTH + GDN_WIDTH + 2 * GDN_HEADS + 2 * SMLP_WIDTH
SPLITS = (QKV_WIDTH,
          QKV_WIDTH + GDN_WIDTH,
          QKV_WIDTH + GDN_WIDTH + GDN_HEADS,
          QKV_WIDTH + GDN_WIDTH + 2 * GDN_HEADS,
          QKV_WIDTH + GDN_WIDTH + 2 * GDN_HEADS + SMLP_WIDTH)
DN_ALPHA = (2 * DEPTH) ** 0.25
DN_BETA = (8 * DEPTH) ** -0.25
LN_EPS = 1e-5
NORM_EPS = 1e-6

kernel_name = 'hybrid_gdn_smlp_stream_step'


def layer_norm(x, g, b):
    xf = x.astype(jnp.float32)
    mu = xf.mean(-1, keepdims=True)
    var = jnp.square(xf - mu).mean(-1, keepdims=True)
    y = (xf - mu) * lax.rsqrt(var + LN_EPS) * g.astype(jnp.float32) + b.astype(jnp.float32)
    return y.astype(x.dtype)


def l2norm(x):
    return x * lax.rsqrt(jnp.sum(jnp.square(x), -1, keepdims=True) + NORM_EPS)


def causal_conv_silu(x, tail, w):
    t = x.shape[1]
    xc = jnp.concatenate([tail.astype(x.dtype), x], axis=1)
    y = sum(xc[:, j:j + t] * w[j] for j in range(CONV_W))
    return jax.nn.silu(y), xc[:, -(CONV_W - 1):]


def gated_delta_rule(q, k, v, g, beta, s0):
    bsz, t, h, dk = q.shape
    dv = v.shape[-1]
    n = -(-t // CHUNK)
    pad = n * CHUNK - t

    def to_blocks(a):
        a = jnp.pad(a, [(0, 0), (0, pad)] + [(0, 0)] * (a.ndim - 2))
        a = a.reshape((bsz, n, CHUNK) + a.shape[2:])
        return jnp.moveaxis(a, 3, 1)

    qc, kc, vc, gc, bc = (to_blocks(a) for a in (q, k, v, g, beta))
    gcum = jnp.cumsum(gc, axis=-1)
    causal = jnp.tril(jnp.ones((CHUNK, CHUNK), bool))
    strict = jnp.tril(jnp.ones((CHUNK, CHUNK), bool), -1)
    decay = jnp.exp(jnp.where(causal, gcum[..., :, None] - gcum[..., None, :], -jnp.inf))
    kb = kc * bc[..., None]
    a_mat = jnp.where(strict, jnp.einsum('bhncd,bhnsd->bhncs', kb, kc) * decay, 0.0)
    eye = jnp.eye(CHUNK, dtype=jnp.float32)
    rhs = jnp.concatenate([vc * bc[..., None], kb * jnp.exp(gcum)[..., None]], axis=-1)
    sol = lax.linalg.triangular_solve(a_mat + eye, rhs, left_side=True, lower=True,
                                      unit_diagonal=True)
    u_blk, w_blk = sol[..., :dv], sol[..., dv:]
    attn = jnp.einsum('bhncd,bhnsd->bhncs', qc, kc) * decay
    k_dec = kc * jnp.exp(gcum[..., -1:] - gcum)[..., None]
    q_dec = qc * jnp.exp(gcum)[..., None]
    g_last = jnp.exp(gcum[..., -1])
    xs = tuple(jnp.moveaxis(a, 2, 0) for a in (q_dec, k_dec, u_blk, w_blk, attn, g_last))

    def step(s, inp):
        qd, kd, ui, wi, at, gl = inp
        v_new = ui - jnp.einsum('bhcd,bhde->bhce', wi, s)
        o = jnp.einsum('bhcd,bhde->bhce', qd, s) + jnp.einsum('bhcs,bhse->bhce', at, v_new)
        s = s * gl[..., None, None] + jnp.einsum('bhcd,bhce->bhde', kd, v_new)
        return s, o

    s_fin, o = lax.scan(step, s0, xs)
    o = jnp.moveaxis(o, 0, 2).reshape(bsz, h, n * CHUNK, dv)[:, :, :t]
    return jnp.moveaxis(o, 1, 2), s_fin


def gdn_mixer(qkv, z, b_logit, a_logit, conv_tail, s0, conv_w, a_log, dt_bias, norm_g):
    bsz, t, _ = qkv.shape
    f32 = jnp.float32
    qkv_c, new_tail = causal_conv_silu(qkv, conv_tail, conv_w)
    q, k, v = jnp.split(qkv_c.astype(f32), 3, axis=-1)
    shp = (bsz, t, GDN_HEADS, GDN_HEAD_DIM)
    q = l2norm(q.reshape(shp)) * (GDN_HEAD_DIM ** -0.5)
    k = l2norm(k.reshape(shp))
    v = v.reshape(shp)
    beta = jax.nn.sigmoid(b_logit.astype(f32))
    g = -jnp.exp(a_log.astype(f32)) * jax.nn.softplus(a_logit.astype(f32) + dt_bias.astype(f32))
    o, s_fin = gated_delta_rule(q, k, v, g, beta, s0.astype(f32))
    o = o * lax.rsqrt(jnp.mean(jnp.square(o), -1, keepdims=True) + NORM_EPS)
    o = o * norm_g.astype(f32) * jax.nn.silu(z.astype(f32).reshape(shp))
    return o.reshape(bsz, t, GDN_WIDTH).astype(qkv.dtype), new_tail, s_fin.astype(s0.dtype)


def spatial_gating(u, v, ln_g, ln_b, w_s, b_s):
    bsz, t, _ = u.shape
    u = jax.nn.gelu(u, approximate=False)
    v = layer_norm(jax.nn.gelu(v, approximate=False), ln_g, ln_b)
    n = -(-t // SMLP_CHUNK)
    pad = n * SMLP_CHUNK - t
    vc = jnp.pad(v, ((0, 0), (0, pad), (0, 0))).reshape(bsz, n, SMLP_CHUNK, SMLP_GROUPS, SMLP_GROUP_DIM)
    w = jnp.where(jnp.tril(jnp.ones((SMLP_CHUNK, SMLP_CHUNK), bool)), w_s, 0.0)
    s = jnp.einsum('gpq,bnqgc->bnpgc', w, vc) + jnp.transpose(b_s)[None, None, :, :, None]
    s = s.reshape(bsz, n * SMLP_CHUNK, SMLP_WIDTH)[:, :t]
    return u * s, v


def trunk_layer(x, conv_tail, s0, w_in, conv_w, a_log, dt_bias, gdn_norm_g, smlp_ln_g, smlp_ln_b,
                w_s, b_s, w_out, ln1_g, ln1_b, w_up, w_down, ln2_g, ln2_b):
    h = jnp.einsum('btd,de->bte', x, w_in)
    qkv, z, b_logit, a_logit, u, v = jnp.split(h, SPLITS, axis=-1)
    y_a, new_tail, new_s = gdn_mixer(qkv, z, b_logit, a_logit, conv_tail, s0,
                                     conv_w, a_log, dt_bias, gdn_norm_g)
    y_b, v_rows = spatial_gating(u, v, smlp_ln_g, smlp_ln_b, w_s, b_s)
    mix = jnp.einsum('btm,md->btd', jnp.concatenate([y_a, y_b], axis=-1), w_out)
    x = layer_norm(DN_ALPHA * x + mix, ln1_g, ln1_b)
    hid = jnp.square(jax.nn.relu(jnp.einsum('btd,df->btf', x, w_up)))
    x = layer_norm(DN_ALPHA * x + jnp.einsum('btf,fd->btd', hid, w_down), ln2_g, ln2_b)
    return x, new_tail, new_s, v_rows


def setup_inputs(seed: int = 0) -> dict:
    key = jax.random.key(seed)
    ks = jax.random.split(key, 24)
    f32 = jnp.float32

    def nrm(k, shape, scale):
        return jax.random.normal(k, shape, f32) * scale

    dt = jnp.exp(jax.random.uniform(ks[9], (DEPTH, GDN_HEADS), f32, math.log(1e-3), math.log(1e-1)))
    return {
        'x_prompt': nrm(ks[0], (BATCH, SEQ, D_MODEL), 1.0),
        'x_sample': nrm(ks[1], (DEC_BATCH, DEC_SEQ, D_MODEL), 1.0),
        'cache_conv': nrm(ks[2], (DEPTH, DEC_BATCH, CONV_W - 1, QKV_WIDTH), 1.0),
        'state_delta': nrm(ks[3], (DEPTH, DEC_BATCH, GDN_HEADS, GDN_HEAD_DIM, GDN_HEAD_DIM), 0.1),
        'ln0_g': 1.0 + nrm(ks[4], (D_MODEL,), 0.05),
        'ln0_b': nrm(ks[5], (D_MODEL,), 0.02),
        'w_in': nrm(ks[6], (DEPTH, D_MODEL, IN_WIDTH), D_MODEL ** -0.5),
        'conv_w': nrm(ks[7], (DEPTH, CONV_W, QKV_WIDTH), CONV_W ** -0.5),
        'a_log': jnp.log(jax.random.uniform(ks[8], (DEPTH, GDN_HEADS), f32, 1.0, 16.0)),
        'dt_bias': dt + jnp.log(-jnp.expm1(-dt)),
        'gdn_norm_g': 1.0 + nrm(ks[10], (DEPTH, GDN_HEAD_DIM), 0.05),
        'smlp_ln_g': 1.0 + nrm(ks[11], (DEPTH, SMLP_WIDTH), 0.05),
        'smlp_ln_b': nrm(ks[12], (DEPTH, SMLP_WIDTH), 0.02),
        'w_s': nrm(ks[13], (DEPTH, SMLP_GROUPS, SMLP_CHUNK, SMLP_CHUNK), 0.5 * SMLP_CHUNK ** -0.5),
        'b_s': 1.0 + nrm(ks[14], (DEPTH, SMLP_GROUPS, SMLP_CHUNK), 0.01),
        'w_out': nrm(ks[15], (DEPTH, MIX_WIDTH, D_MODEL), DN_BETA * MIX_WIDTH ** -0.5),
        'ln1_g': 1.0 + nrm(ks[16], (DEPTH, D_MODEL), 0.05),
        'ln1_b': nrm(ks[17], (DEPTH, D_MODEL), 0.02),
        'w_up': nrm(ks[18], (DEPTH, D_MODEL, D_FF), D_MODEL ** -0.5),
        'w_down': nrm(ks[19], (DEPTH, D_FF, D_MODEL), DN_BETA * D_FF ** -0.5),
        'ln2_g': 1.0 + nrm(ks[20], (DEPTH, D_MODEL), 0.05),
        'ln2_b': nrm(ks[21], (DEPTH, D_MODEL), 0.02),
    }


def reference(x_prompt, x_sample, cache_conv, state_delta, ln0_g, ln0_b, w_in, conv_w, a_log,
              dt_bias, gdn_norm_g, smlp_ln_g, smlp_ln_b, w_s, b_s, w_out, ln1_g, ln1_b,
              w_up, w_down, ln2_g, ln2_b):
    xp = layer_norm(x_prompt, ln0_g, ln0_b)
    xs = layer_norm(x_sample, ln0_g, ln0_b)
    tail0 = jnp.zeros((x_prompt.shape[0], CONV_W - 1, QKV_WIDTH), x_prompt.dtype)
    s_zero = jnp.zeros((x_prompt.shape[0], GDN_HEADS, GDN_HEAD_DIM, GDN_HEAD_DIM), x_prompt.dtype)
    conv_p, delta_p, conv_s, delta_s, v_s = [], [], [], [], []
    for l in range(DEPTH):
        lw = (w_in[l], conv_w[l], a_log[l], dt_bias[l], gdn_norm_g[l], smlp_ln_g[l], smlp_ln_b[l],
              w_s[l], b_s[l], w_out[l], ln1_g[l], ln1_b[l], w_up[l], w_down[l], ln2_g[l], ln2_b[l])
        xp, tp, sp, _ = trunk_layer(xp, tail0, s_zero, *lw)
        xs, tsm, ssm, vsm = trunk_layer(xs, cache_conv[l], state_delta[l], *lw)
        conv_p.append(tp)
        delta_p.append(sp)
        conv_s.append(tsm)
        delta_s.append(ssm)
        v_s.append(vsm)
    return (xp, xs, jnp.stack(conv_p), jnp.stack(delta_p), jnp.stack(conv_s), jnp.stack(delta_s), jnp.stack(v_s))
```

```python
import functools

import jax
import jax.numpy as jnp
from jax import lax
from jax.experimental import pallas as pl
from jax.experimental.pallas import tpu as pltpu

F32 = jnp.float32
BF16 = jnp.bfloat16

D_MODEL = 1024
GDN_HEADS = 4
HEAD_DIM = 128
GDN_WIDTH = GDN_HEADS * HEAD_DIM
QKV_WIDTH = 3 * GDN_WIDTH
CONV_W = 4
SMLP_GROUPS = 4
SMLP_WIDTH = 512
SMLP_GROUP_DIM = 128
SMLP_CHUNK = 128
D_FF = 4096
DELTA_CHUNK = 64
LOGIT_PAD = 128
TAIL_ROWS = 8
DEPTH = 2
DN_ALPHA = (2 * DEPTH) ** 0.25
LN_EPS = 1e-5
NORM_EPS = 1e-6

V7X_VMEM_LIMIT_BYTES = 56 * 1024 * 1024


def _layer_norm(x, g, b):
    mu = jnp.mean(x, axis=-1, keepdims=True)
    xc = x - mu
    var = jnp.mean(xc * xc, axis=-1, keepdims=True)
    return xc * lax.rsqrt(var + LN_EPS) * g + b


def _gelu(x):
    return 0.5 * x * (1.0 + lax.erf(x * (2.0 ** -0.5)))


def _dot(a, b):
    return jnp.dot(a.astype(BF16), b.astype(BF16), preferred_element_type=F32)


def _dot_nt(a, b):
    return lax.dot_general(a.astype(BF16), b.astype(BF16), (((1,), (1,)), ((), ())),
                           preferred_element_type=F32)


def _const_spec(shape):
    nd = len(shape)
    return pl.BlockSpec(shape, lambda *_: (0,) * nd, pipeline_mode=pl.Buffered(1))


def _inproj_kernel(x_ref, ln0g_ref, ln0b_ref, wqkvz_ref, wlg_ref, wuv_ref, sg_ref, sb_ref, ws_ref,
                   bs_ref, qkv_ref, z_ref, lg_ref, yb_ref, *maybe_vrow_ref, apply_ln0, chunk_len):
    x = x_ref[...]
    if apply_ln0:
        x = _layer_norm(x, ln0g_ref[...], ln0b_ref[...])
    xb = x.astype(BF16)
    h = jnp.dot(xb, wqkvz_ref[...], preferred_element_type=F32)
    qkv_ref[...] = h[:, :QKV_WIDTH]
    z_ref[...] = h[:, QKV_WIDTH:]
    lg_ref[...] = jnp.dot(xb, wlg_ref[...], preferred_element_type=F32)

    uv = jnp.dot(xb, wuv_ref[...], preferred_element_type=F32)
    u = _gelu(uv[:, :SMLP_WIDTH])
    v = _gelu(uv[:, SMLP_WIDTH:])
    v = _layer_norm(v, sg_ref[...], sb_ref[...])
    if maybe_vrow_ref:
        maybe_vrow_ref[0][...] = v
    vb = v.astype(BF16)

    tm = x.shape[0]
    row = lax.broadcasted_iota(jnp.int32, (SMLP_CHUNK, SMLP_CHUNK), 0)
    col = lax.broadcasted_iota(jnp.int32, (SMLP_CHUNK, SMLP_CHUNK), 1)
    lower = col <= row
    for g in range(SMLP_GROUPS):
        wg = jnp.where(lower, ws_ref[g], 0.0).astype(BF16)[:chunk_len, :chunk_len]
        bias = bs_ref[:chunk_len, g * SMLP_GROUP_DIM:(g + 1) * SMLP_GROUP_DIM]
        cs = slice(g * SMLP_GROUP_DIM, (g + 1) * SMLP_GROUP_DIM)
        for c in range(tm // chunk_len):
            rs = slice(c * chunk_len, (c + 1) * chunk_len)
            s = jnp.dot(wg, vb[rs, cs], preferred_element_type=F32) + bias
            yb_ref[rs, cs] = (u[rs, cs] * s).astype(BF16)


def _inproj(x, ln0_g, ln0_b, wqkvz, wlg, wuv, smlp_g, smlp_b, w_s, bs_full, *, apply_ln0, chunk_len,
            tm, want_v):
    n = x.shape[0]
    assert n % tm == 0 and tm % chunk_len == 0
    row_spec = lambda w: pl.BlockSpec((tm, w), lambda i: (i, 0))
    kern = functools.partial(_inproj_kernel, apply_ln0=apply_ln0, chunk_len=chunk_len)
    out_specs = [row_spec(QKV_WIDTH), row_spec(GDN_WIDTH), row_spec(LOGIT_PAD), row_spec(SMLP_WIDTH)]
    out_shape = [
        jax.ShapeDtypeStruct((n, QKV_WIDTH), F32),
        jax.ShapeDtypeStruct((n, GDN_WIDTH), F32),
        jax.ShapeDtypeStruct((n, LOGIT_PAD), F32),
        jax.ShapeDtypeStruct((n, SMLP_WIDTH), BF16),
    ]
    if want_v:
        out_specs.append(row_spec(SMLP_WIDTH))
        out_shape.append(jax.ShapeDtypeStruct((n, SMLP_WIDTH), F32))
    return pl.pallas_call(
        kern,
        grid=(n // tm,),
        in_specs=[
            row_spec(D_MODEL),
            _const_spec((1, D_MODEL)), _const_spec((1, D_MODEL)),
            _const_spec((D_MODEL, QKV_WIDTH + GDN_WIDTH)),
            _const_spec((D_MODEL, LOGIT_PAD)),
            _const_spec((D_MODEL, 2 * SMLP_WIDTH)),
            _const_spec((1, SMLP_WIDTH)), _const_spec((1, SMLP_WIDTH)),
            _const_spec((SMLP_GROUPS, SMLP_CHUNK, SMLP_CHUNK)),
            _const_spec((SMLP_CHUNK, SMLP_WIDTH)),
        ],
        out_specs=out_specs,
        out_shape=out_shape,
        compiler_params=pltpu.CompilerParams(dimension_semantics=("arbitrary",),
                                             vmem_limit_bytes=V7X_VMEM_LIMIT_BYTES),
        name="inproj_smlp",
    )(x, ln0_g, ln0_b, wqkvz, wlg, wuv, smlp_g, smlp_b, w_s, bs_full)


def _unit_lower_inverse(a):
    c = a.shape[0]
    row = lax.broadcasted_iota(jnp.int32, (c, c), 0)
    col = lax.broadcasted_iota(jnp.int32, (c, c), 1)
    t = jnp.where(row == col, 1.0, 0.0).astype(F32) - a
    p = a
    span = 2
    while span < c:
        p = _dot(p, p)
        t = t + _dot(t, p)
        span *= 2
    return t


def _gdn_kernel(qkv_ref, z_ref, lg_ref, tail0_ref, s0_ref, convw_ref, hp_ref, ng_ref,
                ya_ref, tail_ref, sfin_ref, xc_ref, qkvc_ref, s_ref, *, tb):
    t_idx = pl.program_id(1)
    n_t = pl.num_programs(1)
    C = DELTA_CHUNK

    @pl.when(t_idx == 0)
    def _():
        xc_ref[0:TAIL_ROWS, :] = tail0_ref[0]
        s_ref[...] = s0_ref[0]

    xc_ref[TAIL_ROWS:TAIL_ROWS + tb, :] = qkv_ref[...]
    base = TAIL_ROWS - (CONV_W - 1)
    acc = xc_ref[base:base + tb, :] * convw_ref[0:1, :]
    for j in range(1, CONV_W):
        acc = acc + xc_ref[base + j:base + j + tb, :] * convw_ref[j:j + 1, :]
    qkvc_ref[...] = acc * jax.nn.sigmoid(acc)
    new_tail = xc_ref[tb:tb + TAIL_ROWS, :]
    xc_ref[0:TAIL_ROWS, :] = new_tail

    @pl.when(t_idx == n_t - 1)
    def _():
        tail_ref[0] = new_tail

    lg = lg_ref[...]
    beta_all = jax.nn.sigmoid(lg)
    g_all = -jnp.exp(hp_ref[0:1, :]) * jax.nn.softplus(lg + hp_ref[1:2, :])
    rowc = lax.broadcasted_iota(jnp.int32, (tb, LOGIT_PAD), 0) % C
    gcum_all = g_all
    sh = 1
    while sh < C:
        gcum_all = gcum_all + jnp.where(rowc >= sh, pltpu.roll(gcum_all, sh, axis=0), 0.0)
        sh *= 2

    row = lax.broadcasted_iota(jnp.int32, (C, C), 0)
    col = lax.broadcasted_iota(jnp.int32, (C, C), 1)
    causal = col <= row
    strict = col < row
    ng = ng_ref[...]

    for c in range(tb // C):
        rs = slice(c * C, (c + 1) * C)
        gcum_c = gcum_all[rs, :]
        gcum_t = gcum_c.T
        beta_c = beta_all[rs, :]
        for h in range(GDN_HEADS):
            hs = slice(h * HEAD_DIM, (h + 1) * HEAD_DIM)
            q = qkvc_ref[rs, h * HEAD_DIM:(h + 1) * HEAD_DIM]
            k = qkvc_ref[rs, GDN_WIDTH + h * HEAD_DIM:GDN_WIDTH + (h + 1) * HEAD_DIM]
            v = qkvc_ref[rs, 2 * GDN_WIDTH + h * HEAD_DIM:2 * GDN_WIDTH + (h + 1) * HEAD_DIM]
            q = q * lax.rsqrt(jnp.sum(q * q, axis=-1, keepdims=True) + NORM_EPS) * (HEAD_DIM ** -0.5)
            k = k * lax.rsqrt(jnp.sum(k * k, axis=-1, keepdims=True) + NORM_EPS)
            beta = beta_c[:, h:h + 1]
            gc = gcum_c[:, GDN_HEADS + h:GDN_HEADS + h + 1]
            grow = gcum_t[GDN_HEADS + h:GDN_HEADS + h + 1, :]
            decay = jnp.exp(jnp.where(causal, gc - grow, -jnp.inf))
            kb = k * beta
            a_mat = jnp.where(strict, _dot_nt(kb, k) * decay, 0.0)
            t_inv = _unit_lower_inverse(a_mat)
            eg = jnp.exp(gc)
            rhs = jnp.concatenate([v * beta, kb * eg], axis=-1)
            sol = _dot(t_inv, rhs)
            u_blk = sol[:, :HEAD_DIM]
            w_blk = sol[:, HEAD_DIM:]
            attn = _dot_nt(q, k) * decay
            g_last = gc[C - 1:C, :]
            k_dec = k * jnp.exp(g_last - gc)
            q_dec = q * eg

            s = s_ref[h]
            r = _dot(jnp.concatenate([w_blk, q_dec], axis=0), s)
            v_new = u_blk - r[:C]
            o = r[C:] + _dot(attn, v_new)
            s_ref[h] = s * jnp.exp(g_last) + _dot(k_dec.T, v_new)

            o = o * lax.rsqrt(jnp.mean(o * o, axis=-1, keepdims=True) + NORM_EPS)
            zc = z_ref[rs, hs]
            o = o * ng * (zc * jax.nn.sigmoid(zc))
            ya_ref[rs, hs] = o.astype(BF16)

    @pl.when(t_idx == n_t - 1)
    def _():
        sfin_ref[0] = s_ref[...]


def _gdn(qkv, z, lg, tail0, s0, convw, hp, ng, *, bsz, t, tb):
    assert t % tb == 0 and tb % DELTA_CHUNK == 0
    nt = t // tb
    row_spec = lambda w: pl.BlockSpec((tb, w), lambda b, i: (b * nt + i, 0))
    kern = functools.partial(_gdn_kernel, tb=tb)
    return pl.pallas_call(
        kern,
        grid=(bsz, nt),
        in_specs=[
            row_spec(QKV_WIDTH), row_spec(GDN_WIDTH), row_spec(LOGIT_PAD),
            pl.BlockSpec((1, TAIL_ROWS, QKV_WIDTH), lambda b, i: (b, 0, 0)),
            pl.BlockSpec((1, GDN_HEADS, HEAD_DIM, HEAD_DIM), lambda b, i: (b, 0, 0, 0)),
            _const_spec((TAIL_ROWS, QKV_WIDTH)),
            _const_spec((TAIL_ROWS, LOGIT_PAD)),
            _const_spec((1, HEAD_DIM)),
        ],
        out_specs=[
            row_spec(GDN_WIDTH),
            pl.BlockSpec((1, TAIL_ROWS, QKV_WIDTH), lambda b, i: (b, 0, 0)),
            pl.BlockSpec((1, GDN_HEADS, HEAD_DIM, HEAD_DIM), lambda b, i: (b, 0, 0, 0)),
        ],
        out_shape=[
            jax.ShapeDtypeStruct((bsz * t, GDN_WIDTH), BF16),
            jax.ShapeDtypeStruct((bsz, TAIL_ROWS, QKV_WIDTH), F32),
            jax.ShapeDtypeStruct((bsz, GDN_HEADS, HEAD_DIM, HEAD_DIM), F32),
        ],
        scratch_shapes=[
            pltpu.VMEM((tb + TAIL_ROWS, QKV_WIDTH), F32),
            pltpu.VMEM((tb, QKV_WIDTH), F32),
            pltpu.VMEM((GDN_HEADS, HEAD_DIM, HEAD_DIM), F32),
        ],
        compiler_params=pltpu.CompilerParams(dimension_semantics=("arbitrary", "arbitrary"),
                                             vmem_limit_bytes=V7X_VMEM_LIMIT_BYTES),
        name="gdn",
    )(qkv, z, lg, tail0, s0, convw, hp, ng)


def _out_ffn_kernel(x_ref, ya_ref, yb_ref, ln0g_ref, ln0b_ref, woa_ref, wob_ref, ln1g_ref, ln1b_ref,
                    wup_ref, wdn_ref, ln2g_ref, ln2b_ref, o_ref, *, apply_ln0):
    x = x_ref[...]
    if apply_ln0:
        x = _layer_norm(x, ln0g_ref[...], ln0b_ref[...])
    mix = (jnp.dot(ya_ref[...], woa_ref[...], preferred_element_type=F32)
           + jnp.dot(yb_ref[...], wob_ref[...], preferred_element_type=F32))
    x1 = _layer_norm(DN_ALPHA * x + mix, ln1g_ref[...], ln1b_ref[...])
    hid = jnp.dot(x1.astype(BF16), wup_ref[...], preferred_element_type=F32)
    hid = jnp.square(jnp.maximum(hid, 0.0)).astype(BF16)
    ffn = jnp.dot(hid, wdn_ref[...], preferred_element_type=F32)
    o_ref[...] = _layer_norm(DN_ALPHA * x1 + ffn, ln2g_ref[...], ln2b_ref[...])


def _out_ffn(x, ya, yb, ln0_g, ln0_b, woa, wob, ln1_g, ln1_b, wup, wdn, ln2_g, ln2_b, *, apply_ln0, tm):
    n = x.shape[0]
    assert n % tm == 0
    row_spec = lambda w: pl.BlockSpec((tm, w), lambda i: (i, 0))
    vec = _const_spec((1, D_MODEL))
    kern = functools.partial(_out_ffn_kernel, apply_ln0=apply_ln0)
    return pl.pallas_call(
        kern,
        grid=(n // tm,),
        in_specs=[
            row_spec(D_MODEL), row_spec(GDN_WIDTH), row_spec(SMLP_WIDTH),
            vec, vec,
            _const_spec((GDN_WIDTH, D_MODEL)), _const_spec((SMLP_WIDTH, D_MODEL)),
            vec, vec,
            _const_spec((D_MODEL, D_FF)), _const_spec((D_FF, D_MODEL)),
            vec, vec,
        ],
        out_specs=row_spec(D_MODEL),
        out_shape=jax.ShapeDtypeStruct((n, D_MODEL), F32),
        compiler_params=pltpu.CompilerParams(dimension_semantics=("arbitrary",),
                                             vmem_limit_bytes=V7X_VMEM_LIMIT_BYTES),
        name="out_ffn",
    )(x, ya, yb, ln0_g, ln0_b, woa, wob, ln1_g, ln1_b, wup, wdn, ln2_g, ln2_b)


def _prep_layer(l, w_in, conv_w, a_log, dt_bias, gdn_norm_g, smlp_ln_g, smlp_ln_b, w_s, b_s, w_out,
                ln1_g, ln1_b, w_up, w_down, ln2_g, ln2_b):
    wi = w_in[l]
    z_end = QKV_WIDTH + GDN_WIDTH
    lg_end = z_end + 2 * GDN_HEADS
    wqkvz = wi[:, :z_end].astype(BF16)
    wlg = jnp.pad(wi[:, z_end:lg_end], ((0, 0), (0, LOGIT_PAD - 2 * GDN_HEADS))).astype(BF16)
    wuv = wi[:, lg_end:].astype(BF16)
    convw = jnp.pad(conv_w[l], ((0, TAIL_ROWS - CONV_W), (0, 0)))
    hp = jnp.zeros((TAIL_ROWS, LOGIT_PAD), F32)
    hp = hp.at[0, GDN_HEADS:2 * GDN_HEADS].set(a_log[l]).at[1, GDN_HEADS:2 * GDN_HEADS].set(dt_bias[l])
    ng = gdn_norm_g[l].reshape(1, HEAD_DIM)
    bs_full = jnp.repeat(jnp.transpose(b_s[l]), SMLP_GROUP_DIM, axis=1)
    return dict(
        wqkvz=wqkvz, wlg=wlg, wuv=wuv, convw=convw, hp=hp, ng=ng,
        smlp_g=smlp_ln_g[l].reshape(1, -1), smlp_b=smlp_ln_b[l].reshape(1, -1),
        w_s=w_s[l], bs_full=bs_full,
        woa=w_out[l, :GDN_WIDTH].astype(BF16), wob=w_out[l, GDN_WIDTH:].astype(BF16),
        ln1_g=ln1_g[l].reshape(1, -1), ln1_b=ln1_b[l].reshape(1, -1),
        wup=w_up[l].astype(BF16), wdn=w_down[l].astype(BF16),
        ln2_g=ln2_g[l].reshape(1, -1), ln2_b=ln2_b[l].reshape(1, -1),
    )


def _trunk_layer(x, tail0, s0, p, ln0_g, ln0_b, *, bsz, t, apply_ln0, tm, tb, want_v):
    chunk_len = min(SMLP_CHUNK, t)
    outs = _inproj(x, ln0_g, ln0_b, p["wqkvz"], p["wlg"], p["wuv"], p["smlp_g"], p["smlp_b"],
                   p["w_s"], p["bs_full"], apply_ln0=apply_ln0, chunk_len=chunk_len, tm=tm,
                   want_v=want_v)
    qkv, z, lg, yb = outs[:4]
    vrows = outs[4] if want_v else None
    tail_pad = jnp.pad(tail0, ((0, 0), (TAIL_ROWS - (CONV_W - 1), 0), (0, 0)))
    ya, tail, s_fin = _gdn(qkv, z, lg, tail_pad, s0, p["convw"], p["hp"], p["ng"], bsz=bsz, t=t, tb=tb)
    x_next = _out_ffn(x, ya, yb, ln0_g, ln0_b, p["woa"], p["wob"], p["ln1_g"], p["ln1_b"], p["wup"],
                      p["wdn"], p["ln2_g"], p["ln2_b"], apply_ln0=apply_ln0, tm=tm)
    return x_next, tail[:, TAIL_ROWS - (CONV_W - 1):], s_fin, vrows


def kernel(x_prompt, x_sample, cache_conv, state_delta, ln0_g, ln0_b, w_in, conv_w, a_log, dt_bias,
           gdn_norm_g, smlp_ln_g, smlp_ln_b, w_s, b_s, w_out, ln1_g, ln1_b, w_up, w_down, ln2_g, ln2_b):
    bp, tp, d = x_prompt.shape
    bs, ts, _ = x_sample.shape
    depth = w_in.shape[0]
    xp = x_prompt.reshape(bp * tp, d)
    xs = x_sample.reshape(bs * ts, d)
    g0 = ln0_g.reshape(1, d)
    b0 = ln0_b.reshape(1, d)
    tail_zero = jnp.zeros((bp, CONV_W - 1, QKV_WIDTH), F32)
    s_zero = jnp.zeros((bp, GDN_HEADS, HEAD_DIM, HEAD_DIM), F32)
    conv_p, delta_p, conv_s, delta_s, v_s = [], [], [], [], []
    for l in range(depth):
        p = _prep_layer(l, w_in, conv_w, a_log, dt_bias, gdn_norm_g, smlp_ln_g, smlp_ln_b, w_s, b_s,
                        w_out, ln1_g, ln1_b, w_up, w_down, ln2_g, ln2_b)
        xp, tl_p, s_p, _ = _trunk_layer(xp, tail_zero, s_zero, p, g0, b0, bsz=bp, t=tp,
                                        apply_ln0=(l == 0), tm=min(512, bp * tp), tb=min(256, tp),
                                        want_v=False)
        xs, tl_s, s_s, vr = _trunk_layer(xs, cache_conv[l], state_delta[l], p, g0, b0, bsz=bs, t=ts,
                                         apply_ln0=(l == 0), tm=min(512, bs * ts), tb=min(256, ts),
                                         want_v=True)
        conv_p.append(tl_p)
        delta_p.append(s_p)
        conv_s.append(tl_s)
        delta_s.append(s_s)
        v_s.append(vr.reshape(bs, ts, SMLP_WIDTH))
    return (xp.reshape(bp, tp, d), xs.reshape(bs, ts, d), jnp.stack(conv_p), jnp.stack(delta_p),
            jnp.stack(conv_s), jnp.stack(delta_s), jnp.stack(v_s))
```

```python
import functools

import jax
import jax.numpy as jnp
from jax import lax
from jax.experimental import pallas as pl
from jax.experimental.pallas import tpu as pltpu

F32 = jnp.float32
BF16 = jnp.bfloat16

D_MODEL = 1024
GDN_HEADS = 4
HEAD_DIM = 128
GDN_WIDTH = GDN_HEADS * HEAD_DIM
QKV_WIDTH = 3 * GDN_WIDTH
CONV_W = 4
SMLP_GROUPS = 4
SMLP_WIDTH = 512
SMLP_GROUP_DIM = 128
SMLP_CHUNK = 128
D_FF = 4096
DELTA_CHUNK = 64
LOGIT_PAD = 128
TAIL_ROWS = 8
DEPTH = 2
DN_ALPHA = (2 * DEPTH) ** 0.25
LN_EPS = 1e-5
NORM_EPS = 1e-6

V7X_VMEM_LIMIT_BYTES = 56 * 1024 * 1024


def _layer_norm(x, g, b):
    mu = jnp.mean(x, axis=-1, keepdims=True)
    xc = x - mu
    var = jnp.mean(xc * xc, axis=-1, keepdims=True)
    return xc * lax.rsqrt(var + LN_EPS) * g + b


def _gelu(x):
    return 0.5 * x * (1.0 + lax.erf(x * (2.0 ** -0.5)))


def _dot(a, b):
    return jnp.dot(a.astype(BF16), b.astype(BF16), preferred_element_type=F32)


def _dot_nt(a, b):
    return lax.dot_general(a.astype(BF16), b.astype(BF16), (((1,), (1,)), ((), ())),
                           preferred_element_type=F32)


def _const_spec(shape):
    nd = len(shape)
    return pl.BlockSpec(shape, lambda *_: (0,) * nd, pipeline_mode=pl.Buffered(1))


def _inproj_kernel(x_ref, ln0g_ref, ln0b_ref, wqkvz_ref, wlg_ref, wuv_ref, sg_ref, sb_ref, ws_ref,
                   bs_ref, qkv_ref, z_ref, lg_ref, yb_ref, *maybe_vrow_ref, apply_ln0, chunk_len):
    x = x_ref[...]
    if apply_ln0:
        x = _layer_norm(x, ln0g_ref[...], ln0b_ref[...])
    xb = x.astype(BF16)
    h = jnp.dot(xb, wqkvz_ref[...], preferred_element_type=F32)
    qkv_ref[...] = h[:, :QKV_WIDTH]
    z_ref[...] = h[:, QKV_WIDTH:]
    lg_ref[...] = jnp.dot(xb, wlg_ref[...], preferred_element_type=F32)

    uv = jnp.dot(xb, wuv_ref[...], preferred_element_type=F32)
    u = _gelu(uv[:, :SMLP_WIDTH])
    v = _gelu(uv[:, SMLP_WIDTH:])
    v = _layer_norm(v, sg_ref[...], sb_ref[...])
    if maybe_vrow_ref:
        maybe_vrow_ref[0][...] = v
    vb = v.astype(BF16)

    tm = x.shape[0]
    row = lax.broadcasted_iota(jnp.int32, (SMLP_CHUNK, SMLP_CHUNK), 0)
    col = lax.broadcasted_iota(jnp.int32, (SMLP_CHUNK, SMLP_CHUNK), 1)
    lower = col <= row
    for g in range(SMLP_GROUPS):
        wg = jnp.where(lower, ws_ref[g], 0.0).astype(BF16)[:chunk_len, :chunk_len]
        bias = bs_ref[:chunk_len, g * SMLP_GROUP_DIM:(g + 1) * SMLP_GROUP_DIM]
        cs = slice(g * SMLP_GROUP_DIM, (g + 1) * SMLP_GROUP_DIM)
        for c in range(tm // chunk_len):
            rs = slice(c * chunk_len, (c + 1) * chunk_len)
            s = jnp.dot(wg, vb[rs, cs], preferred_element_type=F32) + bias
            yb_ref[rs, cs] = (u[rs, cs] * s).astype(BF16)


def _inproj(x, ln0_g, ln0_b, wqkvz, wlg, wuv, smlp_g, smlp_b, w_s, bs_full, *, apply_ln0, chunk_len,
            tm, want_v):
    n = x.shape[0]
    assert n % tm == 0 and tm % chunk_len == 0
    row_spec = lambda w: pl.BlockSpec((tm, w), lambda i: (i, 0))
    kern = functools.partial(_inproj_kernel, apply_ln0=apply_ln0, chunk_len=chunk_len)
    out_specs = [row_spec(QKV_WIDTH), row_spec(GDN_WIDTH), row_spec(LOGIT_PAD), row_spec(SMLP_WIDTH)]
    out_shape = [
        jax.ShapeDtypeStruct((n, QKV_WIDTH), F32),
        jax.ShapeDtypeStruct((n, GDN_WIDTH), F32),
        jax.ShapeDtypeStruct((n, LOGIT_PAD), F32),
        jax.ShapeDtypeStruct((n, SMLP_WIDTH), BF16),
    ]
    if want_v:
        out_specs.append(row_spec(SMLP_WIDTH))
        out_shape.append(jax.ShapeDtypeStruct((n, SMLP_WIDTH), F32))
    return pl.pallas_call(
        kern,
        grid=(n // tm,),
        in_specs=[
            row_spec(D_MODEL),
            _const_spec((1, D_MODEL)), _const_spec((1, D_MODEL)),
            _const_spec((D_MODEL, QKV_WIDTH + GDN_WIDTH)),
            _const_spec((D_MODEL, LOGIT_PAD)),
            _const_spec((D_MODEL, 2 * SMLP_WIDTH)),
            _const_spec((1, SMLP_WIDTH)), _const_spec((1, SMLP_WIDTH)),
            _const_spec((SMLP_GROUPS, SMLP_CHUNK, SMLP_CHUNK)),
            _const_spec((SMLP_CHUNK, SMLP_WIDTH)),
        ],
        out_specs=out_specs,
        out_shape=out_shape,
        compiler_params=pltpu.CompilerParams(dimension_semantics=("arbitrary",),
                                             vmem_limit_bytes=V7X_VMEM_LIMIT_BYTES),
        name="inproj_smlp",
    )(x, ln0_g, ln0_b, wqkvz, wlg, wuv, smlp_g, smlp_b, w_s, bs_full)


def _gdn_kernel(qkv_ref, z_ref, lg_ref, tail0_ref, s0_ref, convw_ref, hp_ref, ng_ref,
                ya_ref, tail_ref, sfin_ref, xc_ref, qkvc_ref, s_ref, *, tb):
    t_idx = pl.program_id(1)
    n_t = pl.num_programs(1)
    C = DELTA_CHUNK

    @pl.when(t_idx == 0)
    def _():
        xc_ref[0:TAIL_ROWS, :] = tail0_ref[0]
        s_ref[...] = s0_ref[0]

    xc_ref[TAIL_ROWS:TAIL_ROWS + tb, :] = qkv_ref[...]
    base = TAIL_ROWS - (CONV_W - 1)
    acc = xc_ref[base:base + tb, :] * convw_ref[0:1, :]
    for j in range(1, CONV_W):
        acc = acc + xc_ref[base + j:base + j + tb, :] * convw_ref[j:j + 1, :]
    qkvc_ref[...] = acc * jax.nn.sigmoid(acc)
    new_tail = xc_ref[tb:tb + TAIL_ROWS, :]
    xc_ref[0:TAIL_ROWS, :] = new_tail

    @pl.when(t_idx == n_t - 1)
    def _():
        tail_ref[0] = new_tail

    lg = lg_ref[...]
    beta_all = jax.nn.sigmoid(lg)
    g_all = -jnp.exp(hp_ref[0:1, :]) * jax.nn.softplus(lg + hp_ref[1:2, :])
    rowc = lax.broadcasted_iota(jnp.int32, (tb, LOGIT_PAD), 0) % C
    gcum_all = g_all
    sh = 1
    while sh < C:
        gcum_all = gcum_all + jnp.where(rowc >= sh, pltpu.roll(gcum_all, sh, axis=0), 0.0)
        sh *= 2

    row = lax.broadcasted_iota(jnp.int32, (C, C), 0)
    col = lax.broadcasted_iota(jnp.int32, (C, C), 1)
    causal = col <= row
    strict = col < row
    ng = ng_ref[...]

    eye = jnp.where(row == col, 1.0, 0.0).astype(F32)
    n_chunks = tb // C
    items = [(c, h) for c in range(n_chunks) for h in range(GDN_HEADS)]

    st = {}
    for c in range(n_chunks):
        rs = slice(c * C, (c + 1) * C)
        gcum_c = gcum_all[rs, :]
        gcum_t = gcum_c.T
        beta_c = beta_all[rs, :]
        for h in range(GDN_HEADS):
            q = qkvc_ref[rs, h * HEAD_DIM:(h + 1) * HEAD_DIM]
            k = qkvc_ref[rs, GDN_WIDTH + h * HEAD_DIM:GDN_WIDTH + (h + 1) * HEAD_DIM]
            v = qkvc_ref[rs, 2 * GDN_WIDTH + h * HEAD_DIM:2 * GDN_WIDTH + (h + 1) * HEAD_DIM]
            q = q * lax.rsqrt(jnp.sum(q * q, axis=-1, keepdims=True) + NORM_EPS) * (HEAD_DIM ** -0.5)
            k = k * lax.rsqrt(jnp.sum(k * k, axis=-1, keepdims=True) + NORM_EPS)
            beta = beta_c[:, h:h + 1]
            gc = gcum_c[:, GDN_HEADS + h:GDN_HEADS + h + 1]
            grow = gcum_t[GDN_HEADS + h:GDN_HEADS + h + 1, :]
            decay = jnp.exp(jnp.where(causal, gc - grow, -jnp.inf))
            kb = k * beta
            eg = jnp.exp(gc)
            g_last = gc[C - 1:C, :]
            st[c, h] = dict(
                q=q, k=k, kb=kb, decay=decay,
                rhs=jnp.concatenate([v * beta, kb * eg], axis=-1),
                k_dec_t=(k * jnp.exp(g_last - gc)).T,
                q_dec=q * eg, s_scale=jnp.exp(g_last))

    for it in items:
        d = st[it]
        both = _dot_nt(jnp.concatenate([d["kb"], d["q"]], axis=0), d["k"])
        d["p"] = jnp.where(strict, both[:C] * d["decay"], 0.0)
        d["attn"] = both[C:] * d["decay"]
        d["t"] = eye - d["p"]
    span = 1
    while span < C:
        last = 2 * span >= C
        for it in items:
            d = st[it]
            if span == 1:
                d["p"] = _dot(d["p"], d["p"])
            elif last:
                d["t"] = d["t"] + _dot(d["t"], d["p"])
            else:
                both = _dot(jnp.concatenate([d["p"], d["t"]], axis=0), d["p"])
                d["t"] = d["t"] + both[C:]
                d["p"] = both[:C]
        span *= 2
    for it in items:
        d = st[it]
        sol = _dot(d["t"], d["rhs"])
        d["u"] = sol[:, :HEAD_DIM]
        d["w"] = sol[:, HEAD_DIM:]

    for c in range(n_chunks):
        rs = slice(c * C, (c + 1) * C)
        s_old = [s_ref[h] for h in range(GDN_HEADS)]
        r = [_dot(jnp.concatenate([st[c, h]["w"], st[c, h]["q_dec"]], axis=0), s_old[h])
             for h in range(GDN_HEADS)]
        v_new = [st[c, h]["u"] - r[h][:C] for h in range(GDN_HEADS)]
        ds = [_dot(st[c, h]["k_dec_t"], v_new[h]) for h in range(GDN_HEADS)]
        oa = [_dot(st[c, h]["attn"], v_new[h]) for h in range(GDN_HEADS)]
        for h in range(GDN_HEADS):
            hs = slice(h * HEAD_DIM, (h + 1) * HEAD_DIM)
            s_ref[h] = s_old[h] * st[c, h]["s_scale"] + ds[h]
            o = r[h][C:] + oa[h]
            o = o * lax.rsqrt(jnp.mean(o * o, axis=-1, keepdims=True) + NORM_EPS)
            zc = z_ref[rs, hs]
            o = o * ng * (zc * jax.nn.sigmoid(zc))
            ya_ref[rs, hs] = o.astype(BF16)

    @pl.when(t_idx == n_t - 1)
    def _():
        sfin_ref[0] = s_ref[...]


def _gdn(qkv, z, lg, tail0, s0, convw, hp, ng, *, bsz, t, tb):
    assert t % tb == 0 and tb % DELTA_CHUNK == 0
    nt = t // tb
    row_spec = lambda w: pl.BlockSpec((tb, w), lambda b, i: (b * nt + i, 0))
    kern = functools.partial(_gdn_kernel, tb=tb)
    return pl.pallas_call(
        kern,
        grid=(bsz, nt),
        in_specs=[
            row_spec(QKV_WIDTH), row_spec(GDN_WIDTH), row_spec(LOGIT_PAD),
            pl.BlockSpec((1, TAIL_ROWS, QKV_WIDTH), lambda b, i: (b, 0, 0)),
            pl.BlockSpec((1, GDN_HEADS, HEAD_DIM, HEAD_DIM), lambda b, i: (b, 0, 0, 0)),
            _const_spec((TAIL_ROWS, QKV_WIDTH)),
            _const_spec((TAIL_ROWS, LOGIT_PAD)),
            _const_spec((1, HEAD_DIM)),
        ],
        out_specs=[
            row_spec(GDN_WIDTH),
            pl.BlockSpec((1, TAIL_ROWS, QKV_WIDTH), lambda b, i: (b, 0, 0)),
            pl.BlockSpec((1, GDN_HEADS, HEAD_DIM, HEAD_DIM), lambda b, i: (b, 0, 0, 0)),
        ],
        out_shape=[
            jax.ShapeDtypeStruct((bsz * t, GDN_WIDTH), BF16),
            jax.ShapeDtypeStruct((bsz, TAIL_ROWS, QKV_WIDTH), F32),
            jax.ShapeDtypeStruct((bsz, GDN_HEADS, HEAD_DIM, HEAD_DIM), F32),
        ],
        scratch_shapes=[
            pltpu.VMEM((tb + TAIL_ROWS, QKV_WIDTH), F32),
            pltpu.VMEM((tb, QKV_WIDTH), F32),
            pltpu.VMEM((GDN_HEADS, HEAD_DIM, HEAD_DIM), F32),
        ],
        compiler_params=pltpu.CompilerParams(dimension_semantics=("arbitrary", "arbitrary"),
                                             vmem_limit_bytes=V7X_VMEM_LIMIT_BYTES),
        name="gdn",
    )(qkv, z, lg, tail0, s0, convw, hp, ng)


def _out_ffn_kernel(x_ref, ya_ref, yb_ref, ln0g_ref, ln0b_ref, woa_ref, wob_ref, ln1g_ref, ln1b_ref,
                    wup_ref, wdn_ref, ln2g_ref, ln2b_ref, o_ref, *, apply_ln0):
    x = x_ref[...]
    if apply_ln0:
        x = _layer_norm(x, ln0g_ref[...], ln0b_ref[...])
    mix = (jnp.dot(ya_ref[...], woa_ref[...], preferred_element_type=F32)
           + jnp.dot(yb_ref[...], wob_ref[...], preferred_element_type=F32))
    x1 = _layer_norm(DN_ALPHA * x + mix, ln1g_ref[...], ln1b_ref[...])
    hid = jnp.dot(x1.astype(BF16), wup_ref[...], preferred_element_type=F32)
    hid = jnp.square(jnp.maximum(hid, 0.0)).astype(BF16)
    ffn = jnp.dot(hid, wdn_ref[...], preferred_element_type=F32)
    o_ref[...] = _layer_norm(DN_ALPHA * x1 + ffn, ln2g_ref[...], ln2b_ref[...])


def _out_ffn(x, ya, yb, ln0_g, ln0_b, woa, wob, ln1_g, ln1_b, wup, wdn, ln2_g, ln2_b, *, apply_ln0, tm):
    n = x.shape[0]
    assert n % tm == 0
    row_spec = lambda w: pl.BlockSpec((tm, w), lambda i: (i, 0))
    vec = _const_spec((1, D_MODEL))
    kern = functools.partial(_out_ffn_kernel, apply_ln0=apply_ln0)
    return pl.pallas_call(
        kern,
        grid=(n // tm,),
        in_specs=[
            row_spec(D_MODEL), row_spec(GDN_WIDTH), row_spec(SMLP_WIDTH),
            vec, vec,
            _const_spec((GDN_WIDTH, D_MODEL)), _const_spec((SMLP_WIDTH, D_MODEL)),
            vec, vec,
            _const_spec((D_MODEL, D_FF)), _const_spec((D_FF, D_MODEL)),
            vec, vec,
        ],
        out_specs=row_spec(D_MODEL),
        out_shape=jax.ShapeDtypeStruct((n, D_MODEL), F32),
        compiler_params=pltpu.CompilerParams(dimension_semantics=("arbitrary",),
                                             vmem_limit_bytes=V7X_VMEM_LIMIT_BYTES),
        name="out_ffn",
    )(x, ya, yb, ln0_g, ln0_b, woa, wob, ln1_g, ln1_b, wup, wdn, ln2_g, ln2_b)


def _prep_layer(l, w_in, conv_w, a_log, dt_bias, gdn_norm_g, smlp_ln_g, smlp_ln_b, w_s, b_s, w_out,
                ln1_g, ln1_b, w_up, w_down, ln2_g, ln2_b):
    wi = w_in[l]
    z_end = QKV_WIDTH + GDN_WIDTH
    lg_end = z_end + 2 * GDN_HEADS
    wqkvz = wi[:, :z_end].astype(BF16)
    wlg = jnp.pad(wi[:, z_end:lg_end], ((0, 0), (0, LOGIT_PAD - 2 * GDN_HEADS))).astype(BF16)
    wuv = wi[:, lg_end:].astype(BF16)
    convw = jnp.pad(conv_w[l], ((0, TAIL_ROWS - CONV_W), (0, 0)))
    hp = jnp.zeros((TAIL_ROWS, LOGIT_PAD), F32)
    hp = hp.at[0, GDN_HEADS:2 * GDN_HEADS].set(a_log[l]).at[1, GDN_HEADS:2 * GDN_HEADS].set(dt_bias[l])
    ng = gdn_norm_g[l].reshape(1, HEAD_DIM)
    bs_full = jnp.repeat(jnp.transpose(b_s[l]), SMLP_GROUP_DIM, axis=1)
    return dict(
        wqkvz=wqkvz, wlg=wlg, wuv=wuv, convw=convw, hp=hp, ng=ng,
        smlp_g=smlp_ln_g[l].reshape(1, -1), smlp_b=smlp_ln_b[l].reshape(1, -1),
        w_s=w_s[l], bs_full=bs_full,
        woa=w_out[l, :GDN_WIDTH].astype(BF16), wob=w_out[l, GDN_WIDTH:].astype(BF16),
        ln1_g=ln1_g[l].reshape(1, -1), ln1_b=ln1_b[l].reshape(1, -1),
        wup=w_up[l].astype(BF16), wdn=w_down[l].astype(BF16),
        ln2_g=ln2_g[l].reshape(1, -1), ln2_b=ln2_b[l].reshape(1, -1),
    )


def _trunk_layer(x, tail0, s0, p, ln0_g, ln0_b, *, bsz, t, apply_ln0, tm, tb, want_v):
    chunk_len = min(SMLP_CHUNK, t)
    outs = _inproj(x, ln0_g, ln0_b, p["wqkvz"], p["wlg"], p["wuv"], p["smlp_g"], p["smlp_b"],
                   p["w_s"], p["bs_full"], apply_ln0=apply_ln0, chunk_len=chunk_len, tm=tm,
                   want_v=want_v)
    qkv, z, lg, yb = outs[:4]
    vrows = outs[4] if want_v else None
    tail_pad = jnp.pad(tail0, ((0, 0), (TAIL_ROWS - (CONV_W - 1), 0), (0, 0)))
    ya, tail, s_fin = _gdn(qkv, z, lg, tail_pad, s0, p["convw"], p["hp"], p["ng"], bsz=bsz, t=t, tb=tb)
    x_next = _out_ffn(x, ya, yb, ln0_g, ln0_b, p["woa"], p["wob"], p["ln1_g"], p["ln1_b"], p["wup"],
                      p["wdn"], p["ln2_g"], p["ln2_b"], apply_ln0=apply_ln0, tm=tm)
    return x_next, tail[:, TAIL_ROWS - (CONV_W - 1):], s_fin, vrows


def kernel(x_prompt, x_sample, cache_conv, state_delta, ln0_g, ln0_b, w_in, conv_w, a_log, dt_bias,
           gdn_norm_g, smlp_ln_g, smlp_ln_b, w_s, b_s, w_out, ln1_g, ln1_b, w_up, w_down, ln2_g, ln2_b):
    bp, tp, d = x_prompt.shape
    bs, ts, _ = x_sample.shape
    depth = w_in.shape[0]
    xp = x_prompt.reshape(bp * tp, d)
    xs = x_sample.reshape(bs * ts, d)
    g0 = ln0_g.reshape(1, d)
    b0 = ln0_b.reshape(1, d)
    tail_zero = jnp.zeros((bp, CONV_W - 1, QKV_WIDTH), F32)
    s_zero = jnp.zeros((bp, GDN_HEADS, HEAD_DIM, HEAD_DIM), F32)
    conv_p, delta_p, conv_s, delta_s, v_s = [], [], [], [], []
    for l in range(depth):
        p = _prep_layer(l, w_in, conv_w, a_log, dt_bias, gdn_norm_g, smlp_ln_g, smlp_ln_b, w_s, b_s,
                        w_out, ln1_g, ln1_b, w_up, w_down, ln2_g, ln2_b)
        xp, tl_p, s_p, _ = _trunk_layer(xp, tail_zero, s_zero, p, g0, b0, bsz=bp, t=tp,
                                        apply_ln0=(l == 0), tm=min(512, bp * tp), tb=min(256, tp),
                                        want_v=False)
        xs, tl_s, s_s, vr = _trunk_layer(xs, cache_conv[l], state_delta[l], p, g0, b0, bsz=bs, t=ts,
                                         apply_ln0=(l == 0), tm=min(512, bs * ts), tb=min(256, ts),
                                         want_v=True)
        conv_p.append(tl_p)
        delta_p.append(s_p)
        conv_s.append(tl_s)
        delta_s.append(s_s)
        v_s.append(vr.reshape(bs, ts, SMLP_WIDTH))
    return (xp.reshape(bp, tp, d), xs.reshape(bs, ts, d), jnp.stack(conv_p), jnp.stack(delta_p),
            jnp.stack(conv_s), jnp.stack(delta_s), jnp.stack(v_s))
```

```python
import functools

import jax
import jax.numpy as jnp
from jax import lax
from jax.experimental import pallas as pl
from jax.experimental.pallas import tpu as pltpu

F32 = jnp.float32
BF16 = jnp.bfloat16

D_MODEL = 1024
GDN_HEADS = 4
HEAD_DIM = 128
GDN_WIDTH = GDN_HEADS * HEAD_DIM
QKV_WIDTH = 3 * GDN_WIDTH
CONV_W = 4
SMLP_GROUPS = 4
SMLP_WIDTH = 512
SMLP_GROUP_DIM = 128
SMLP_CHUNK = 128
D_FF = 4096
DELTA_CHUNK = 64
LOGIT_PAD = 128
TAIL_ROWS = 8
DEPTH = 2
DN_ALPHA = (2 * DEPTH) ** 0.25
LN_EPS = 1e-5
NORM_EPS = 1e-6

V7X_VMEM_LIMIT_BYTES = 56 * 1024 * 1024

INPROJ_ROWS = 1024
INPROJ_ROW_GROUPS = 2
GDN_ROWS = 512
FFN_ROWS = 512
FFN_ROW_GROUPS = 2


def _layer_norm(x, g, b):
    mu = jnp.mean(x, axis=-1, keepdims=True)
    xc = x - mu
    var = jnp.mean(xc * xc, axis=-1, keepdims=True)
    return xc * lax.rsqrt(var + LN_EPS) * g + b


def _gelu(x):
    return 0.5 * x * (1.0 + lax.erf(x * (2.0 ** -0.5)))


def _dot(a, b):
    return jnp.dot(a.astype(BF16), b.astype(BF16), preferred_element_type=F32)


def _dot_nt(a, b):
    return lax.dot_general(a.astype(BF16), b.astype(BF16), (((1,), (1,)), ((), ())),
                           preferred_element_type=F32)


def _const_spec(shape):
    nd = len(shape)
    return pl.BlockSpec(shape, lambda *_: (0,) * nd, pipeline_mode=pl.Buffered(1))


def _inproj_kernel(x_ref, ln0g_ref, ln0b_ref, wqkvz_ref, wlg_ref, wuv_ref, sg_ref, sb_ref, ws_ref,
                   bs_ref, qkv_ref, z_ref, lg_ref, yb_ref, *maybe_vrow_ref, apply_ln0, chunk_len, n_sub):
    tm = x_ref.shape[0]
    sub = tm // n_sub
    uv_parts = []
    for i in range(n_sub):
        rs = slice(i * sub, (i + 1) * sub)
        x = x_ref[rs, :]
        if apply_ln0:
            x = _layer_norm(x, ln0g_ref[...], ln0b_ref[...])
        xb = x.astype(BF16)
        h = jnp.dot(xb, wqkvz_ref[...], preferred_element_type=F32)
        qkv_ref[rs, :] = h[:, :QKV_WIDTH]
        z_ref[rs, :] = h[:, QKV_WIDTH:]
        lg_ref[rs, :] = jnp.dot(xb, wlg_ref[...], preferred_element_type=F32)
        uv_parts.append(jnp.dot(xb, wuv_ref[...], preferred_element_type=F32))

    row = lax.broadcasted_iota(jnp.int32, (SMLP_CHUNK, SMLP_CHUNK), 0)
    col = lax.broadcasted_iota(jnp.int32, (SMLP_CHUNK, SMLP_CHUNK), 1)
    lower = col <= row
    wgs = [jnp.where(lower, ws_ref[g], 0.0).astype(BF16)[:chunk_len, :chunk_len]
           for g in range(SMLP_GROUPS)]
    for i in range(n_sub):
        uv = uv_parts[i]
        u = _gelu(uv[:, :SMLP_WIDTH])
        v = _gelu(uv[:, SMLP_WIDTH:])
        v = _layer_norm(v, sg_ref[...], sb_ref[...])
        if maybe_vrow_ref:
            maybe_vrow_ref[0][i * sub:(i + 1) * sub, :] = v
        vb = v.astype(BF16)
        for g in range(SMLP_GROUPS):
            bias = bs_ref[:chunk_len, g * SMLP_GROUP_DIM:(g + 1) * SMLP_GROUP_DIM]
            cs = slice(g * SMLP_GROUP_DIM, (g + 1) * SMLP_GROUP_DIM)
            for c in range(sub // chunk_len):
                rs = slice(c * chunk_len, (c + 1) * chunk_len)
                s = jnp.dot(wgs[g], vb[rs, cs], preferred_element_type=F32) + bias
                yb_ref[i * sub + c * chunk_len:i * sub + (c + 1) * chunk_len, cs] = (u[rs, cs] * s).astype(BF16)


def _inproj(x, ln0_g, ln0_b, wqkvz, wlg, wuv, smlp_g, smlp_b, w_s, bs_full, *, apply_ln0, chunk_len,
            tm, n_sub, want_v):
    n = x.shape[0]
    assert n % tm == 0 and tm % n_sub == 0 and (tm // n_sub) % chunk_len == 0
    row_spec = lambda w: pl.BlockSpec((tm, w), lambda i: (i, 0))
    kern = functools.partial(_inproj_kernel, apply_ln0=apply_ln0, chunk_len=chunk_len, n_sub=n_sub)
    out_specs = [row_spec(QKV_WIDTH), row_spec(GDN_WIDTH), row_spec(LOGIT_PAD), row_spec(SMLP_WIDTH)]
    out_shape = [
        jax.ShapeDtypeStruct((n, QKV_WIDTH), F32),
        jax.ShapeDtypeStruct((n, GDN_WIDTH), F32),
        jax.ShapeDtypeStruct((n, LOGIT_PAD), F32),
        jax.ShapeDtypeStruct((n, SMLP_WIDTH), BF16),
    ]
    if want_v:
        out_specs.append(row_spec(SMLP_WIDTH))
        out_shape.append(jax.ShapeDtypeStruct((n, SMLP_WIDTH), F32))
    return pl.pallas_call(
        kern,
        grid=(n // tm,),
        in_specs=[
            row_spec(D_MODEL),
            _const_spec((1, D_MODEL)), _const_spec((1, D_MODEL)),
            _const_spec((D_MODEL, QKV_WIDTH + GDN_WIDTH)),
            _const_spec((D_MODEL, LOGIT_PAD)),
            _const_spec((D_MODEL, 2 * SMLP_WIDTH)),
            _const_spec((1, SMLP_WIDTH)), _const_spec((1, SMLP_WIDTH)),
            _const_spec((SMLP_GROUPS, SMLP_CHUNK, SMLP_CHUNK)),
            _const_spec((SMLP_CHUNK, SMLP_WIDTH)),
        ],
        out_specs=out_specs,
        out_shape=out_shape,
        compiler_params=pltpu.CompilerParams(dimension_semantics=("arbitrary",),
                                             vmem_limit_bytes=V7X_VMEM_LIMIT_BYTES),
        name="inproj_smlp",
    )(x, ln0_g, ln0_b, wqkvz, wlg, wuv, smlp_g, smlp_b, w_s, bs_full)


def _gdn_kernel(qkv_ref, z_ref, lg_ref, tail0_ref, s0_ref, convw_ref, hp_ref, ng_ref,
                ya_ref, tail_ref, sfin_ref, xc_ref, qkvc_ref, s_ref, *, tb):
    t_idx = pl.program_id(1)
    n_t = pl.num_programs(1)
    C = DELTA_CHUNK

    @pl.when(t_idx == 0)
    def _():
        xc_ref[0:TAIL_ROWS, :] = tail0_ref[0]
        s_ref[...] = s0_ref[0]

    xc_ref[TAIL_ROWS:TAIL_ROWS + tb, :] = qkv_ref[...]
    base = TAIL_ROWS - (CONV_W - 1)
    acc = xc_ref[base:base + tb, :] * convw_ref[0:1, :]
    for j in range(1, CONV_W):
        acc = acc + xc_ref[base + j:base + j + tb, :] * convw_ref[j:j + 1, :]
    qkvc_ref[...] = acc * jax.nn.sigmoid(acc)
    new_tail = xc_ref[tb:tb + TAIL_ROWS, :]
    xc_ref[0:TAIL_ROWS, :] = new_tail

    @pl.when(t_idx == n_t - 1)
    def _():
        tail_ref[0] = new_tail

    lg = lg_ref[...]
    beta_all = jax.nn.sigmoid(lg)
    g_all = -jnp.exp(hp_ref[0:1, :]) * jax.nn.softplus(lg + hp_ref[1:2, :])
    rowc = lax.broadcasted_iota(jnp.int32, (tb, LOGIT_PAD), 0) % C
    gcum_all = g_all
    sh = 1
    while sh < C:
        gcum_all = gcum_all + jnp.where(rowc >= sh, pltpu.roll(gcum_all, sh, axis=0), 0.0)
        sh *= 2

    row = lax.broadcasted_iota(jnp.int32, (C, C), 0)
    col = lax.broadcasted_iota(jnp.int32, (C, C), 1)
    causal = col <= row
    strict = col < row
    ng = ng_ref[...]

    eye = jnp.where(row == col, 1.0, 0.0).astype(F32)
    n_chunks = tb // C
    items = [(c, h) for c in range(n_chunks) for h in range(GDN_HEADS)]

    st = {}
    for c in range(n_chunks):
        rs = slice(c * C, (c + 1) * C)
        gcum_c = gcum_all[rs, :]
        gcum_t = gcum_c.T
        beta_c = beta_all[rs, :]
        for h in range(GDN_HEADS):
            q = qkvc_ref[rs, h * HEAD_DIM:(h + 1) * HEAD_DIM]
            k = qkvc_ref[rs, GDN_WIDTH + h * HEAD_DIM:GDN_WIDTH + (h + 1) * HEAD_DIM]
            v = qkvc_ref[rs, 2 * GDN_WIDTH + h * HEAD_DIM:2 * GDN_WIDTH + (h + 1) * HEAD_DIM]
            q = q * lax.rsqrt(jnp.sum(q * q, axis=-1, keepdims=True) + NORM_EPS) * (HEAD_DIM ** -0.5)
            k = k * lax.rsqrt(jnp.sum(k * k, axis=-1, keepdims=True) + NORM_EPS)
            beta = beta_c[:, h:h + 1]
            gc = gcum_c[:, GDN_HEADS + h:GDN_HEADS + h + 1]
            grow = gcum_t[GDN_HEADS + h:GDN_HEADS + h + 1, :]
            decay = jnp.exp(jnp.where(causal, gc - grow, -jnp.inf))
            kb = k * beta
            eg = jnp.exp(gc)
            g_last = gc[C - 1:C, :]
            st[c, h] = dict(
                q=q, k=k, kb=kb, decay=decay,
                rhs=jnp.concatenate([v * beta, kb * eg], axis=-1),
                k_dec_t=(k * jnp.exp(g_last - gc)).T,
                q_dec=q * eg, s_scale=jnp.exp(g_last))

    for it in items:
        d = st[it]
        both = _dot_nt(jnp.concatenate([d["kb"], d["q"]], axis=0), d["k"])
        d["p"] = jnp.where(strict, both[:C] * d["decay"], 0.0)
        d["attn"] = both[C:] * d["decay"]
        d["t"] = eye - d["p"]
    span = 1
    while span < C:
        last = 2 * span >= C
        for it in items:
            d = st[it]
            if span == 1:
                d["p"] = _dot(d["p"], d["p"])
            elif last:
                d["t"] = d["t"] + _dot(d["t"], d["p"])
            else:
                both = _dot(jnp.concatenate([d["p"], d["t"]], axis=0), d["p"])
                d["t"] = d["t"] + both[C:]
                d["p"] = both[:C]
        span *= 2
    for it in items:
        d = st[it]
        sol = _dot(d["t"], d["rhs"])
        d["u"] = sol[:, :HEAD_DIM]
        d["w"] = sol[:, HEAD_DIM:]

    for c in range(n_chunks):
        rs = slice(c * C, (c + 1) * C)
        s_old = [s_ref[h] for h in range(GDN_HEADS)]
        r = [_dot(jnp.concatenate([st[c, h]["w"], st[c, h]["q_dec"]], axis=0), s_old[h])
             for h in range(GDN_HEADS)]
        v_new = [st[c, h]["u"] - r[h][:C] for h in range(GDN_HEADS)]
        ds = [_dot(st[c, h]["k_dec_t"], v_new[h]) for h in range(GDN_HEADS)]
        oa = [_dot(st[c, h]["attn"], v_new[h]) for h in range(GDN_HEADS)]
        for h in range(GDN_HEADS):
            hs = slice(h * HEAD_DIM, (h + 1) * HEAD_DIM)
            s_ref[h] = s_old[h] * st[c, h]["s_scale"] + ds[h]
            o = r[h][C:] + oa[h]
            o = o * lax.rsqrt(jnp.mean(o * o, axis=-1, keepdims=True) + NORM_EPS)
            zc = z_ref[rs, hs]
            o = o * ng * (zc * jax.nn.sigmoid(zc))
            ya_ref[rs, hs] = o.astype(BF16)

    @pl.when(t_idx == n_t - 1)
    def _():
        sfin_ref[0] = s_ref[...]


def _gdn(qkv, z, lg, tail0, s0, convw, hp, ng, *, bsz, t, tb):
    assert t % tb == 0 and tb % DELTA_CHUNK == 0
    nt = t // tb
    row_spec = lambda w: pl.BlockSpec((tb, w), lambda b, i: (b * nt + i, 0))
    kern = functools.partial(_gdn_kernel, tb=tb)
    return pl.pallas_call(
        kern,
        grid=(bsz, nt),
        in_specs=[
            row_spec(QKV_WIDTH), row_spec(GDN_WIDTH), row_spec(LOGIT_PAD),
            pl.BlockSpec((1, TAIL_ROWS, QKV_WIDTH), lambda b, i: (b, 0, 0)),
            pl.BlockSpec((1, GDN_HEADS, HEAD_DIM, HEAD_DIM), lambda b, i: (b, 0, 0, 0)),
            _const_spec((TAIL_ROWS, QKV_WIDTH)),
            _const_spec((TAIL_ROWS, LOGIT_PAD)),
            _const_spec((1, HEAD_DIM)),
        ],
        out_specs=[
            row_spec(GDN_WIDTH),
            pl.BlockSpec((1, TAIL_ROWS, QKV_WIDTH), lambda b, i: (b, 0, 0)),
            pl.BlockSpec((1, GDN_HEADS, HEAD_DIM, HEAD_DIM), lambda b, i: (b, 0, 0, 0)),
        ],
        out_shape=[
            jax.ShapeDtypeStruct((bsz * t, GDN_WIDTH), BF16),
            jax.ShapeDtypeStruct((bsz, TAIL_ROWS, QKV_WIDTH), F32),
            jax.ShapeDtypeStruct((bsz, GDN_HEADS, HEAD_DIM, HEAD_DIM), F32),
        ],
        scratch_shapes=[
            pltpu.VMEM((tb + TAIL_ROWS, QKV_WIDTH), F32),
            pltpu.VMEM((tb, QKV_WIDTH), F32),
            pltpu.VMEM((GDN_HEADS, HEAD_DIM, HEAD_DIM), F32),
        ],
        compiler_params=pltpu.CompilerParams(dimension_semantics=("arbitrary", "arbitrary"),
                                             vmem_limit_bytes=V7X_VMEM_LIMIT_BYTES),
        name="gdn",
    )(qkv, z, lg, tail0, s0, convw, hp, ng)


def _out_ffn_kernel(x_ref, ya_ref, yb_ref, ln0g_ref, ln0b_ref, woa_ref, wob_ref, ln1g_ref, ln1b_ref,
                    wup_ref, wdn_ref, ln2g_ref, ln2b_ref, o_ref, *, apply_ln0, n_sub):
    sub = x_ref.shape[0] // n_sub
    groups = [slice(i * sub, (i + 1) * sub) for i in range(n_sub)]
    mix = [jnp.dot(ya_ref[rs, :], woa_ref[...], preferred_element_type=F32)
           + jnp.dot(yb_ref[rs, :], wob_ref[...], preferred_element_type=F32) for rs in groups]
    x1, hid = [], []
    for i, rs in enumerate(groups):
        x = x_ref[rs, :]
        if apply_ln0:
            x = _layer_norm(x, ln0g_ref[...], ln0b_ref[...])
        x1.append(_layer_norm(DN_ALPHA * x + mix[i], ln1g_ref[...], ln1b_ref[...]))
        hid.append(jnp.dot(x1[i].astype(BF16), wup_ref[...], preferred_element_type=F32))
    ffn = [jnp.dot(jnp.square(jnp.maximum(h, 0.0)).astype(BF16), wdn_ref[...], preferred_element_type=F32)
           for h in hid]
    for i, rs in enumerate(groups):
        o_ref[rs, :] = _layer_norm(DN_ALPHA * x1[i] + ffn[i], ln2g_ref[...], ln2b_ref[...])


def _out_ffn(x, ya, yb, ln0_g, ln0_b, woa, wob, ln1_g, ln1_b, wup, wdn, ln2_g, ln2_b, *, apply_ln0, tm):
    n = x.shape[0]
    assert n % tm == 0 and tm % FFN_ROW_GROUPS == 0
    row_spec = lambda w: pl.BlockSpec((tm, w), lambda i: (i, 0))
    vec = _const_spec((1, D_MODEL))
    kern = functools.partial(_out_ffn_kernel, apply_ln0=apply_ln0, n_sub=FFN_ROW_GROUPS)
    return pl.pallas_call(
        kern,
        grid=(n // tm,),
        in_specs=[
            row_spec(D_MODEL), row_spec(GDN_WIDTH), row_spec(SMLP_WIDTH),
            vec, vec,
            _const_spec((GDN_WIDTH, D_MODEL)), _const_spec((SMLP_WIDTH, D_MODEL)),
            vec, vec,
            _const_spec((D_MODEL, D_FF)), _const_spec((D_FF, D_MODEL)),
            vec, vec,
        ],
        out_specs=row_spec(D_MODEL),
        out_shape=jax.ShapeDtypeStruct((n, D_MODEL), F32),
        compiler_params=pltpu.CompilerParams(dimension_semantics=("arbitrary",),
                                             vmem_limit_bytes=V7X_VMEM_LIMIT_BYTES),
        name="out_ffn",
    )(x, ya, yb, ln0_g, ln0_b, woa, wob, ln1_g, ln1_b, wup, wdn, ln2_g, ln2_b)


def _prep_layer(l, w_in, conv_w, a_log, dt_bias, gdn_norm_g, smlp_ln_g, smlp_ln_b, w_s, b_s, w_out,
                ln1_g, ln1_b, w_up, w_down, ln2_g, ln2_b):
    wi = w_in[l]
    z_end = QKV_WIDTH + GDN_WIDTH
    lg_end = z_end + 2 * GDN_HEADS
    wqkvz = wi[:, :z_end].astype(BF16)
    wlg = jnp.pad(wi[:, z_end:lg_end], ((0, 0), (0, LOGIT_PAD - 2 * GDN_HEADS))).astype(BF16)
    wuv = wi[:, lg_end:].astype(BF16)
    convw = jnp.pad(conv_w[l], ((0, TAIL_ROWS - CONV_W), (0, 0)))
    hp = jnp.zeros((TAIL_ROWS, LOGIT_PAD), F32)
    hp = hp.at[0, GDN_HEADS:2 * GDN_HEADS].set(a_log[l]).at[1, GDN_HEADS:2 * GDN_HEADS].set(dt_bias[l])
    ng = gdn_norm_g[l].reshape(1, HEAD_DIM)
    bs_full = jnp.repeat(jnp.transpose(b_s[l]), SMLP_GROUP_DIM, axis=1)
    return dict(
        wqkvz=wqkvz, wlg=wlg, wuv=wuv, convw=convw, hp=hp, ng=ng,
        smlp_g=smlp_ln_g[l].reshape(1, -1), smlp_b=smlp_ln_b[l].reshape(1, -1),
        w_s=w_s[l], bs_full=bs_full,
        woa=w_out[l, :GDN_WIDTH].astype(BF16), wob=w_out[l, GDN_WIDTH:].astype(BF16),
        ln1_g=ln1_g[l].reshape(1, -1), ln1_b=ln1_b[l].reshape(1, -1),
        wup=w_up[l].astype(BF16), wdn=w_down[l].astype(BF16),
        ln2_g=ln2_g[l].reshape(1, -1), ln2_b=ln2_b[l].reshape(1, -1),
    )


def _trunk_layer(x, tail0, s0, p, ln0_g, ln0_b, *, bsz, t, apply_ln0, tm, tm_in, n_sub, tb, want_v):
    chunk_len = min(SMLP_CHUNK, t)
    outs = _inproj(x, ln0_g, ln0_b, p["wqkvz"], p["wlg"], p["wuv"], p["smlp_g"], p["smlp_b"],
                   p["w_s"], p["bs_full"], apply_ln0=apply_ln0, chunk_len=chunk_len, tm=tm_in,
                   n_sub=n_sub, want_v=want_v)
    qkv, z, lg, yb = outs[:4]
    vrows = outs[4] if want_v else None
    tail_pad = jnp.pad(tail0, ((0, 0), (TAIL_ROWS - (CONV_W - 1), 0), (0, 0)))
    ya, tail, s_fin = _gdn(qkv, z, lg, tail_pad, s0, p["convw"], p["hp"], p["ng"], bsz=bsz, t=t, tb=tb)
    x_next = _out_ffn(x, ya, yb, ln0_g, ln0_b, p["woa"], p["wob"], p["ln1_g"], p["ln1_b"], p["wup"],
                      p["wdn"], p["ln2_g"], p["ln2_b"], apply_ln0=apply_ln0, tm=tm)
    return x_next, tail[:, TAIL_ROWS - (CONV_W - 1):], s_fin, vrows


def kernel(x_prompt, x_sample, cache_conv, state_delta, ln0_g, ln0_b, w_in, conv_w, a_log, dt_bias,
           gdn_norm_g, smlp_ln_g, smlp_ln_b, w_s, b_s, w_out, ln1_g, ln1_b, w_up, w_down, ln2_g, ln2_b):
    bp, tp, d = x_prompt.shape
    bs, ts, _ = x_sample.shape
    depth = w_in.shape[0]
    xp = x_prompt.reshape(bp * tp, d)
    xs = x_sample.reshape(bs * ts, d)
    g0 = ln0_g.reshape(1, d)
    b0 = ln0_b.reshape(1, d)
    tail_zero = jnp.zeros((bp, CONV_W - 1, QKV_WIDTH), F32)
    s_zero = jnp.zeros((bp, GDN_HEADS, HEAD_DIM, HEAD_DIM), F32)
    conv_p, delta_p, conv_s, delta_s, v_s = [], [], [], [], []
    n_p, n_s = bp * tp, bs * ts
    for l in range(depth):
        p = _prep_layer(l, w_in, conv_w, a_log, dt_bias, gdn_norm_g, smlp_ln_g, smlp_ln_b, w_s, b_s,
                        w_out, ln1_g, ln1_b, w_up, w_down, ln2_g, ln2_b)
        xp, tl_p, s_p, _ = _trunk_layer(xp, tail_zero, s_zero, p, g0, b0, bsz=bp, t=tp,
                                        apply_ln0=(l == 0), tm=min(FFN_ROWS, n_p),
                                        tm_in=min(INPROJ_ROWS, n_p), n_sub=INPROJ_ROW_GROUPS,
                                        tb=min(GDN_ROWS, tp), want_v=False)
        xs, tl_s, s_s, vr = _trunk_layer(xs, cache_conv[l], state_delta[l], p, g0, b0, bsz=bs, t=ts,
                                         apply_ln0=(l == 0), tm=min(FFN_ROWS, n_s),
                                         tm_in=min(INPROJ_ROWS, n_s), n_sub=INPROJ_ROW_GROUPS,
                                         tb=min(GDN_ROWS, ts), want_v=True)
        conv_p.append(tl_p)
        delta_p.append(s_p)
        conv_s.append(tl_s)
        delta_s.append(s_s)
        v_s.append(vr.reshape(bs, ts, SMLP_WIDTH))
    return (xp.reshape(bp, tp, d), xs.reshape(bs, ts, d), jnp.stack(conv_p), jnp.stack(delta_p),
            jnp.stack(conv_s), jnp.stack(delta_s), jnp.stack(v_s))
```

```python
import functools

import jax
import jax.numpy as jnp
from jax import lax
from jax.experimental import pallas as pl
from jax.experimental.pallas import tpu as pltpu

F32 = jnp.float32
BF16 = jnp.bfloat16

D_MODEL = 1024
GDN_HEADS = 4
HEAD_DIM = 128
GDN_WIDTH = GDN_HEADS * HEAD_DIM
QKV_WIDTH = 3 * GDN_WIDTH
CONV_W = 4
SMLP_GROUPS = 4
SMLP_WIDTH = 512
SMLP_GROUP_DIM = 128
SMLP_CHUNK = 128
D_FF = 4096
DELTA_CHUNK = 64
LOGIT_PAD = 128
TAIL_ROWS = 8
DEPTH = 2
DN_ALPHA = (2 * DEPTH) ** 0.25
LN_EPS = 1e-5
NORM_EPS = 1e-6

V7X_VMEM_LIMIT_BYTES = 56 * 1024 * 1024

INPROJ_ROWS = 1024
INPROJ_ROW_GROUPS = 2
GDN_ROWS = 512
GDN_ROW_GROUPS = 2
FFN_ROWS = 512
FFN_ROW_GROUPS = 2


def _layer_norm(x, g, b):
    mu = jnp.mean(x, axis=-1, keepdims=True)
    xc = x - mu
    var = jnp.mean(xc * xc, axis=-1, keepdims=True)
    return xc * lax.rsqrt(var + LN_EPS) * g + b


def _gelu(x):
    return 0.5 * x * (1.0 + lax.erf(x * (2.0 ** -0.5)))


def _dot(a, b):
    return jnp.dot(a.astype(BF16), b.astype(BF16), preferred_element_type=F32)


def _dot_nt(a, b):
    return lax.dot_general(a.astype(BF16), b.astype(BF16), (((1,), (1,)), ((), ())),
                           preferred_element_type=F32)


def _const_spec(shape):
    nd = len(shape)
    return pl.BlockSpec(shape, lambda *_: (0,) * nd, pipeline_mode=pl.Buffered(1))


def _inproj_kernel(x_ref, ln0g_ref, ln0b_ref, wqkvz_ref, wlg_ref, wuv_ref, sg_ref, sb_ref, ws_ref,
                   bs_ref, qkv_ref, z_ref, lg_ref, yb_ref, *maybe_vrow_ref, apply_ln0, chunk_len, n_sub):
    tm = x_ref.shape[0]
    sub = tm // n_sub
    uv_parts = []
    for i in range(n_sub):
        rs = slice(i * sub, (i + 1) * sub)
        x = x_ref[rs, :]
        if apply_ln0:
            x = _layer_norm(x, ln0g_ref[...], ln0b_ref[...])
        xb = x.astype(BF16)
        h = jnp.dot(xb, wqkvz_ref[...], preferred_element_type=F32)
        qkv_ref[rs, :] = h[:, :QKV_WIDTH]
        z_ref[rs, :] = h[:, QKV_WIDTH:]
        lg_ref[rs, :] = jnp.dot(xb, wlg_ref[...], preferred_element_type=F32)
        uv_parts.append(jnp.dot(xb, wuv_ref[...], preferred_element_type=F32))

    row = lax.broadcasted_iota(jnp.int32, (SMLP_CHUNK, SMLP_CHUNK), 0)
    col = lax.broadcasted_iota(jnp.int32, (SMLP_CHUNK, SMLP_CHUNK), 1)
    lower = col <= row
    wgs = [jnp.where(lower, ws_ref[g], 0.0).astype(BF16)[:chunk_len, :chunk_len]
           for g in range(SMLP_GROUPS)]
    for i in range(n_sub):
        uv = uv_parts[i]
        u = _gelu(uv[:, :SMLP_WIDTH])
        v = _gelu(uv[:, SMLP_WIDTH:])
        v = _layer_norm(v, sg_ref[...], sb_ref[...])
        if maybe_vrow_ref:
            maybe_vrow_ref[0][i * sub:(i + 1) * sub, :] = v
        vb = v.astype(BF16)
        for g in range(SMLP_GROUPS):
            bias = bs_ref[:chunk_len, g * SMLP_GROUP_DIM:(g + 1) * SMLP_GROUP_DIM]
            cs = slice(g * SMLP_GROUP_DIM, (g + 1) * SMLP_GROUP_DIM)
            for c in range(sub // chunk_len):
                rs = slice(c * chunk_len, (c + 1) * chunk_len)
                s = jnp.dot(wgs[g], vb[rs, cs], preferred_element_type=F32) + bias
                yb_ref[i * sub + c * chunk_len:i * sub + (c + 1) * chunk_len, cs] = (u[rs, cs] * s).astype(BF16)


def _inproj(x, ln0_g, ln0_b, wqkvz, wlg, wuv, smlp_g, smlp_b, w_s, bs_full, *, apply_ln0, chunk_len,
            tm, n_sub, want_v):
    n = x.shape[0]
    assert n % tm == 0 and tm % n_sub == 0 and (tm // n_sub) % chunk_len == 0
    row_spec = lambda w: pl.BlockSpec((tm, w), lambda i: (i, 0))
    kern = functools.partial(_inproj_kernel, apply_ln0=apply_ln0, chunk_len=chunk_len, n_sub=n_sub)
    out_specs = [row_spec(QKV_WIDTH), row_spec(GDN_WIDTH), row_spec(LOGIT_PAD), row_spec(SMLP_WIDTH)]
    out_shape = [
        jax.ShapeDtypeStruct((n, QKV_WIDTH), F32),
        jax.ShapeDtypeStruct((n, GDN_WIDTH), F32),
        jax.ShapeDtypeStruct((n, LOGIT_PAD), F32),
        jax.ShapeDtypeStruct((n, SMLP_WIDTH), BF16),
    ]
    if want_v:
        out_specs.append(row_spec(SMLP_WIDTH))
        out_shape.append(jax.ShapeDtypeStruct((n, SMLP_WIDTH), F32))
    return pl.pallas_call(
        kern,
        grid=(n // tm,),
        in_specs=[
            row_spec(D_MODEL),
            _const_spec((1, D_MODEL)), _const_spec((1, D_MODEL)),
            _const_spec((D_MODEL, QKV_WIDTH + GDN_WIDTH)),
            _const_spec((D_MODEL, LOGIT_PAD)),
            _const_spec((D_MODEL, 2 * SMLP_WIDTH)),
            _const_spec((1, SMLP_WIDTH)), _const_spec((1, SMLP_WIDTH)),
            _const_spec((SMLP_GROUPS, SMLP_CHUNK, SMLP_CHUNK)),
            _const_spec((SMLP_CHUNK, SMLP_WIDTH)),
        ],
        out_specs=out_specs,
        out_shape=out_shape,
        compiler_params=pltpu.CompilerParams(dimension_semantics=("arbitrary",),
                                             vmem_limit_bytes=V7X_VMEM_LIMIT_BYTES),
        name="inproj_smlp",
    )(x, ln0_g, ln0_b, wqkvz, wlg, wuv, smlp_g, smlp_b, w_s, bs_full)


def _gdn_kernel(qkv_ref, z_ref, lg_ref, tail0_ref, s0_ref, convw_ref, hp_ref, ng_ref,
                ya_ref, tail_ref, sfin_ref, xc_ref, qkvc_ref, s_ref, *, tb, n_sub):
    t_idx = pl.program_id(1)
    n_t = pl.num_programs(1)

    @pl.when(t_idx == 0)
    def _():
        xc_ref[0:TAIL_ROWS, :] = tail0_ref[0]
        s_ref[...] = s0_ref[0]

    xc_ref[TAIL_ROWS:TAIL_ROWS + tb, :] = qkv_ref[...]
    sub = tb // n_sub
    for i in range(n_sub):
        _gdn_rows(i * sub, sub, xc_ref, convw_ref, qkvc_ref, lg_ref, hp_ref, ng_ref, z_ref, s_ref, ya_ref)
    new_tail = xc_ref[tb:tb + TAIL_ROWS, :]
    xc_ref[0:TAIL_ROWS, :] = new_tail

    @pl.when(t_idx == n_t - 1)
    def _():
        tail_ref[0] = new_tail
        sfin_ref[0] = s_ref[...]


def _gdn_rows(r0, nr, xc_ref, convw_ref, qkvc_ref, lg_ref, hp_ref, ng_ref, z_ref, s_ref, ya_ref):
    C = DELTA_CHUNK
    base = TAIL_ROWS - (CONV_W - 1) + r0
    acc = xc_ref[base:base + nr, :] * convw_ref[0:1, :]
    for j in range(1, CONV_W):
        acc = acc + xc_ref[base + j:base + j + nr, :] * convw_ref[j:j + 1, :]
    qkvc_ref[r0:r0 + nr, :] = acc * jax.nn.sigmoid(acc)

    lg = lg_ref[r0:r0 + nr, :]
    beta_all = jax.nn.sigmoid(lg)
    g_all = -jnp.exp(hp_ref[0:1, :]) * jax.nn.softplus(lg + hp_ref[1:2, :])
    rowc = lax.broadcasted_iota(jnp.int32, (nr, LOGIT_PAD), 0) % C
    gcum_all = g_all
    sh = 1
    while sh < C:
        gcum_all = gcum_all + jnp.where(rowc >= sh, pltpu.roll(gcum_all, sh, axis=0), 0.0)
        sh *= 2

    row = lax.broadcasted_iota(jnp.int32, (C, C), 0)
    col = lax.broadcasted_iota(jnp.int32, (C, C), 1)
    causal = col <= row
    strict = col < row
    ng = ng_ref[...]

    eye = jnp.where(row == col, 1.0, 0.0).astype(F32)
    n_chunks = nr // C
    items = [(c, h) for c in range(n_chunks) for h in range(GDN_HEADS)]

    st = {}
    for c in range(n_chunks):
        rs = slice(c * C, (c + 1) * C)
        rr = slice(r0 + c * C, r0 + (c + 1) * C)
        gcum_c = gcum_all[rs, :]
        gcum_t = gcum_c.T
        beta_c = beta_all[rs, :]
        for h in range(GDN_HEADS):
            q = qkvc_ref[rr, h * HEAD_DIM:(h + 1) * HEAD_DIM]
            k = qkvc_ref[rr, GDN_WIDTH + h * HEAD_DIM:GDN_WIDTH + (h + 1) * HEAD_DIM]
            v = qkvc_ref[rr, 2 * GDN_WIDTH + h * HEAD_DIM:2 * GDN_WIDTH + (h + 1) * HEAD_DIM]
            q = q * lax.rsqrt(jnp.sum(q * q, axis=-1, keepdims=True) + NORM_EPS) * (HEAD_DIM ** -0.5)
            k = k * lax.rsqrt(jnp.sum(k * k, axis=-1, keepdims=True) + NORM_EPS)
            beta = beta_c[:, h:h + 1]
            gc = gcum_c[:, GDN_HEADS + h:GDN_HEADS + h + 1]
            grow = gcum_t[GDN_HEADS + h:GDN_HEADS + h + 1, :]
            decay = jnp.exp(jnp.where(causal, gc - grow, -jnp.inf))
            kb = k * beta
            eg = jnp.exp(gc)
            g_last = gc[C - 1:C, :]
            st[c, h] = dict(
                q=q, k=k, kb=kb, decay=decay,
                rhs=jnp.concatenate([v * beta, kb * eg], axis=-1),
                k_dec_t=(k * jnp.exp(g_last - gc)).T,
                q_dec=q * eg, s_scale=jnp.exp(g_last))

    for it in items:
        d = st[it]
        both = _dot_nt(jnp.concatenate([d["kb"], d["q"]], axis=0), d["k"])
        d["p"] = jnp.where(strict, both[:C] * d["decay"], 0.0)
        d["attn"] = both[C:] * d["decay"]
        d["t"] = eye - d["p"]
    span = 1
    while span < C:
        last = 2 * span >= C
        for it in items:
            d = st[it]
            if span == 1:
                d["p"] = _dot(d["p"], d["p"])
            elif last:
                d["t"] = d["t"] + _dot(d["t"], d["p"])
            else:
                both = _dot(jnp.concatenate([d["p"], d["t"]], axis=0), d["p"])
                d["t"] = d["t"] + both[C:]
                d["p"] = both[:C]
        span *= 2
    for it in items:
        d = st[it]
        sol = _dot(d["t"], d["rhs"])
        d["u"] = sol[:, :HEAD_DIM]
        d["w"] = sol[:, HEAD_DIM:]

    for c in range(n_chunks):
        rr = slice(r0 + c * C, r0 + (c + 1) * C)
        s_old = [s_ref[h] for h in range(GDN_HEADS)]
        r = [_dot(jnp.concatenate([st[c, h]["w"], st[c, h]["q_dec"]], axis=0), s_old[h])
             for h in range(GDN_HEADS)]
        v_new = [st[c, h]["u"] - r[h][:C] for h in range(GDN_HEADS)]
        ds = [_dot(st[c, h]["k_dec_t"], v_new[h]) for h in range(GDN_HEADS)]
        oa = [_dot(st[c, h]["attn"], v_new[h]) for h in range(GDN_HEADS)]
        for h in range(GDN_HEADS):
            hs = slice(h * HEAD_DIM, (h + 1) * HEAD_DIM)
            s_ref[h] = s_old[h] * st[c, h]["s_scale"] + ds[h]
            o = r[h][C:] + oa[h]
            o = o * lax.rsqrt(jnp.mean(o * o, axis=-1, keepdims=True) + NORM_EPS)
            zc = z_ref[rr, hs]
            o = o * ng * (zc * jax.nn.sigmoid(zc))
            ya_ref[rr, hs] = o.astype(BF16)


def _gdn(qkv, z, lg, tail0, s0, convw, hp, ng, *, bsz, t, tb):
    n_sub = GDN_ROW_GROUPS if tb % (GDN_ROW_GROUPS * DELTA_CHUNK) == 0 else 1
    assert t % tb == 0 and tb % (n_sub * DELTA_CHUNK) == 0
    nt = t // tb
    row_spec = lambda w: pl.BlockSpec((tb, w), lambda b, i: (b * nt + i, 0))
    kern = functools.partial(_gdn_kernel, tb=tb, n_sub=n_sub)
    return pl.pallas_call(
        kern,
        grid=(bsz, nt),
        in_specs=[
            row_spec(QKV_WIDTH), row_spec(GDN_WIDTH), row_spec(LOGIT_PAD),
            pl.BlockSpec((1, TAIL_ROWS, QKV_WIDTH), lambda b, i: (b, 0, 0)),
            pl.BlockSpec((1, GDN_HEADS, HEAD_DIM, HEAD_DIM), lambda b, i: (b, 0, 0, 0)),
            _const_spec((TAIL_ROWS, QKV_WIDTH)),
            _const_spec((TAIL_ROWS, LOGIT_PAD)),
            _const_spec((1, HEAD_DIM)),
        ],
        out_specs=[
            row_spec(GDN_WIDTH),
            pl.BlockSpec((1, TAIL_ROWS, QKV_WIDTH), lambda b, i: (b, 0, 0)),
            pl.BlockSpec((1, GDN_HEADS, HEAD_DIM, HEAD_DIM), lambda b, i: (b, 0, 0, 0)),
        ],
        out_shape=[
            jax.ShapeDtypeStruct((bsz * t, GDN_WIDTH), BF16),
            jax.ShapeDtypeStruct((bsz, TAIL_ROWS, QKV_WIDTH), F32),
            jax.ShapeDtypeStruct((bsz, GDN_HEADS, HEAD_DIM, HEAD_DIM), F32),
        ],
        scratch_shapes=[
            pltpu.VMEM((tb + TAIL_ROWS, QKV_WIDTH), F32),
            pltpu.VMEM((tb, QKV_WIDTH), F32),
            pltpu.VMEM((GDN_HEADS, HEAD_DIM, HEAD_DIM), F32),
        ],
        compiler_params=pltpu.CompilerParams(dimension_semantics=("arbitrary", "arbitrary"),
                                             vmem_limit_bytes=V7X_VMEM_LIMIT_BYTES),
        name="gdn",
    )(qkv, z, lg, tail0, s0, convw, hp, ng)


def _out_ffn_kernel(x_ref, ya_ref, yb_ref, ln0g_ref, ln0b_ref, woa_ref, wob_ref, ln1g_ref, ln1b_ref,
                    wup_ref, wdn_ref, ln2g_ref, ln2b_ref, o_ref, *, apply_ln0, n_sub):
    sub = x_ref.shape[0] // n_sub
    groups = [slice(i * sub, (i + 1) * sub) for i in range(n_sub)]
    mix = [jnp.dot(ya_ref[rs, :], woa_ref[...], preferred_element_type=F32)
           + jnp.dot(yb_ref[rs, :], wob_ref[...], preferred_element_type=F32) for rs in groups]
    x1, hid = [], []
    for i, rs in enumerate(groups):
        x = x_ref[rs, :]
        if apply_ln0:
            x = _layer_norm(x, ln0g_ref[...], ln0b_ref[...])
        x1.append(_layer_norm(DN_ALPHA * x + mix[i], ln1g_ref[...], ln1b_ref[...]))
        hid.append(jnp.dot(x1[i].astype(BF16), wup_ref[...], preferred_element_type=F32))
    ffn = [jnp.dot(jnp.square(jnp.maximum(h, 0.0)).astype(BF16), wdn_ref[...], preferred_element_type=F32)
           for h in hid]
    for i, rs in enumerate(groups):
        o_ref[rs, :] = _layer_norm(DN_ALPHA * x1[i] + ffn[i], ln2g_ref[...], ln2b_ref[...])


def _out_ffn(x, ya, yb, ln0_g, ln0_b, woa, wob, ln1_g, ln1_b, wup, wdn, ln2_g, ln2_b, *, apply_ln0, tm):
    n = x.shape[0]
    assert n % tm == 0 and tm % FFN_ROW_GROUPS == 0
    row_spec = lambda w: pl.BlockSpec((tm, w), lambda i: (i, 0))
    vec = _const_spec((1, D_MODEL))
    kern = functools.partial(_out_ffn_kernel, apply_ln0=apply_ln0, n_sub=FFN_ROW_GROUPS)
    return pl.pallas_call(
        kern,
        grid=(n // tm,),
        in_specs=[
            row_spec(D_MODEL), row_spec(GDN_WIDTH), row_spec(SMLP_WIDTH),
            vec, vec,
            _const_spec((GDN_WIDTH, D_MODEL)), _const_spec((SMLP_WIDTH, D_MODEL)),
            vec, vec,
            _const_spec((D_MODEL, D_FF)), _const_spec((D_FF, D_MODEL)),
            vec, vec,
        ],
        out_specs=row_spec(D_MODEL),
        out_shape=jax.ShapeDtypeStruct((n, D_MODEL), F32),
        compiler_params=pltpu.CompilerParams(dimension_semantics=("arbitrary",),
                                             vmem_limit_bytes=V7X_VMEM_LIMIT_BYTES),
        name="out_ffn",
    )(x, ya, yb, ln0_g, ln0_b, woa, wob, ln1_g, ln1_b, wup, wdn, ln2_g, ln2_b)


def _prep_layer(l, w_in, conv_w, a_log, dt_bias, gdn_norm_g, smlp_ln_g, smlp_ln_b, w_s, b_s, w_out,
                ln1_g, ln1_b, w_up, w_down, ln2_g, ln2_b):
    wi = w_in[l]
    z_end = QKV_WIDTH + GDN_WIDTH
    lg_end = z_end + 2 * GDN_HEADS
    wqkvz = wi[:, :z_end].astype(BF16)
    wlg = jnp.pad(wi[:, z_end:lg_end], ((0, 0), (0, LOGIT_PAD - 2 * GDN_HEADS))).astype(BF16)
    wuv = wi[:, lg_end:].astype(BF16)
    convw = jnp.pad(conv_w[l], ((0, TAIL_ROWS - CONV_W), (0, 0)))
    hp = jnp.zeros((TAIL_ROWS, LOGIT_PAD), F32)
    hp = hp.at[0, GDN_HEADS:2 * GDN_HEADS].set(a_log[l]).at[1, GDN_HEADS:2 * GDN_HEADS].set(dt_bias[l])
    ng = gdn_norm_g[l].reshape(1, HEAD_DIM)
    bs_full = jnp.repeat(jnp.transpose(b_s[l]), SMLP_GROUP_DIM, axis=1)
    return dict(
        wqkvz=wqkvz, wlg=wlg, wuv=wuv, convw=convw, hp=hp, ng=ng,
        smlp_g=smlp_ln_g[l].reshape(1, -1), smlp_b=smlp_ln_b[l].reshape(1, -1),
        w_s=w_s[l], bs_full=bs_full,
        woa=w_out[l, :GDN_WIDTH].astype(BF16), wob=w_out[l, GDN_WIDTH:].astype(BF16),
        ln1_g=ln1_g[l].reshape(1, -1), ln1_b=ln1_b[l].reshape(1, -1),
        wup=w_up[l].astype(BF16), wdn=w_down[l].astype(BF16),
        ln2_g=ln2_g[l].reshape(1, -1), ln2_b=ln2_b[l].reshape(1, -1),
    )


def _trunk_layer(x, tail0, s0, p, ln0_g, ln0_b, *, bsz, t, apply_ln0, tm, tm_in, n_sub, tb, want_v):
    chunk_len = min(SMLP_CHUNK, t)
    outs = _inproj(x, ln0_g, ln0_b, p["wqkvz"], p["wlg"], p["wuv"], p["smlp_g"], p["smlp_b"],
                   p["w_s"], p["bs_full"], apply_ln0=apply_ln0, chunk_len=chunk_len, tm=tm_in,
                   n_sub=n_sub, want_v=want_v)
    qkv, z, lg, yb = outs[:4]
    vrows = outs[4] if want_v else None
    tail_pad = jnp.pad(tail0, ((0, 0), (TAIL_ROWS - (CONV_W - 1), 0), (0, 0)))
    ya, tail, s_fin = _gdn(qkv, z, lg, tail_pad, s0, p["convw"], p["hp"], p["ng"], bsz=bsz, t=t, tb=tb)
    x_next = _out_ffn(x, ya, yb, ln0_g, ln0_b, p["woa"], p["wob"], p["ln1_g"], p["ln1_b"], p["wup"],
                      p["wdn"], p["ln2_g"], p["ln2_b"], apply_ln0=apply_ln0, tm=tm)
    return x_next, tail[:, TAIL_ROWS - (CONV_W - 1):], s_fin, vrows


def kernel(x_prompt, x_sample, cache_conv, state_delta, ln0_g, ln0_b, w_in, conv_w, a_log, dt_bias,
           gdn_norm_g, smlp_ln_g, smlp_ln_b, w_s, b_s, w_out, ln1_g, ln1_b, w_up, w_down, ln2_g, ln2_b):
    bp, tp, d = x_prompt.shape
    bs, ts, _ = x_sample.shape
    depth = w_in.shape[0]
    xp = x_prompt.reshape(bp * tp, d)
    xs = x_sample.reshape(bs * ts, d)
    g0 = ln0_g.reshape(1, d)
    b0 = ln0_b.reshape(1, d)
    tail_zero = jnp.zeros((bp, CONV_W - 1, QKV_WIDTH), F32)
    s_zero = jnp.zeros((bp, GDN_HEADS, HEAD_DIM, HEAD_DIM), F32)
    conv_p, delta_p, conv_s, delta_s, v_s = [], [], [], [], []
    n_p, n_s = bp * tp, bs * ts
    for l in range(depth):
        p = _prep_layer(l, w_in, conv_w, a_log, dt_bias, gdn_norm_g, smlp_ln_g, smlp_ln_b, w_s, b_s,
                        w_out, ln1_g, ln1_b, w_up, w_down, ln2_g, ln2_b)
        xp, tl_p, s_p, _ = _trunk_layer(xp, tail_zero, s_zero, p, g0, b0, bsz=bp, t=tp,
                                        apply_ln0=(l == 0), tm=min(FFN_ROWS, n_p),
                                        tm_in=min(INPROJ_ROWS, n_p), n_sub=INPROJ_ROW_GROUPS,
                                        tb=min(GDN_ROWS, tp), want_v=False)
        xs, tl_s, s_s, vr = _trunk_layer(xs, cache_conv[l], state_delta[l], p, g0, b0, bsz=bs, t=ts,
                                         apply_ln0=(l == 0), tm=min(FFN_ROWS, n_s),
                                         tm_in=min(INPROJ_ROWS, n_s), n_sub=INPROJ_ROW_GROUPS,
                                         tb=min(GDN_ROWS, ts), want_v=True)
        conv_p.append(tl_p)
        delta_p.append(s_p)
        conv_s.append(tl_s)
        delta_s.append(s_s)
        v_s.append(vr.reshape(bs, ts, SMLP_WIDTH))
    return (xp.reshape(bp, tp, d), xs.reshape(bs, ts, d), jnp.stack(conv_p), jnp.stack(delta_p),
            jnp.stack(conv_s), jnp.stack(delta_s), jnp.stack(v_s))
```

```python
import functools

import jax
import jax.numpy as jnp
from jax import lax
from jax.experimental import pallas as pl
from jax.experimental.pallas import tpu as pltpu

F32 = jnp.float32
BF16 = jnp.bfloat16

D_MODEL = 1024
GDN_HEADS = 4
HEAD_DIM = 128
GDN_WIDTH = GDN_HEADS * HEAD_DIM
QKV_WIDTH = 3 * GDN_WIDTH
CONV_W = 4
SMLP_GROUPS = 4
SMLP_WIDTH = 512
SMLP_GROUP_DIM = 128
SMLP_CHUNK = 128
D_FF = 4096
DELTA_CHUNK = 64
LOGIT_PAD = 128
TAIL_ROWS = 8
DEPTH = 2
DN_ALPHA = (2 * DEPTH) ** 0.25
LN_EPS = 1e-5
NORM_EPS = 1e-6

V7X_VMEM_LIMIT_BYTES = 56 * 1024 * 1024

INPROJ_ROWS = 1024
INPROJ_ROW_GROUPS = 2
GDN_ROWS = 512
GDN_ROW_GROUPS = 2
FFN_ROWS = 512
FFN_ROW_GROUPS = 2


def _layer_norm(x, g, b):
    mu = jnp.mean(x, axis=-1, keepdims=True)
    xc = x - mu
    var = jnp.mean(xc * xc, axis=-1, keepdims=True)
    return xc * lax.rsqrt(var + LN_EPS) * g + b


def _gelu(x):
    return 0.5 * x * (1.0 + lax.erf(x * (2.0 ** -0.5)))


def _dot(a, b):
    return jnp.dot(a.astype(BF16), b.astype(BF16), preferred_element_type=F32)


def _dot_nt(a, b):
    return lax.dot_general(a.astype(BF16), b.astype(BF16), (((1,), (1,)), ((), ())),
                           preferred_element_type=F32)


def _const_spec(shape):
    nd = len(shape)
    return pl.BlockSpec(shape, lambda *_: (0,) * nd, pipeline_mode=pl.Buffered(1))


def _inproj_kernel(x_ref, ln0g_ref, ln0b_ref, wqkvz_ref, wlg_ref, wuv_ref, sg_ref, sb_ref, ws_ref,
                   bs_ref, qkv_ref, z_ref, lg_ref, yb_ref, *maybe_vrow_ref, apply_ln0, chunk_len, n_sub):
    tm = x_ref.shape[0]
    sub = tm // n_sub
    uv_parts = []
    for i in range(n_sub):
        rs = slice(i * sub, (i + 1) * sub)
        x = x_ref[rs, :]
        if apply_ln0:
            x = _layer_norm(x, ln0g_ref[...], ln0b_ref[...])
        xb = x.astype(BF16)
        h = jnp.dot(xb, wqkvz_ref[...], preferred_element_type=F32)
        qkv_ref[rs, :] = h[:, :QKV_WIDTH]
        z_ref[rs, :] = h[:, QKV_WIDTH:]
        lg_ref[rs, :] = jnp.dot(xb, wlg_ref[...], preferred_element_type=F32)
        uv_parts.append(jnp.dot(xb, wuv_ref[...], preferred_element_type=F32))

    row = lax.broadcasted_iota(jnp.int32, (SMLP_CHUNK, SMLP_CHUNK), 0)
    col = lax.broadcasted_iota(jnp.int32, (SMLP_CHUNK, SMLP_CHUNK), 1)
    lower = col <= row
    wgs = [jnp.where(lower, ws_ref[g], 0.0).astype(BF16)[:chunk_len, :chunk_len]
           for g in range(SMLP_GROUPS)]
    for i in range(n_sub):
        uv = uv_parts[i]
        u = _gelu(uv[:, :SMLP_WIDTH])
        v = _gelu(uv[:, SMLP_WIDTH:])
        v = _layer_norm(v, sg_ref[...], sb_ref[...])
        if maybe_vrow_ref:
            maybe_vrow_ref[0][i * sub:(i + 1) * sub, :] = v
        vb = v.astype(BF16)
        for g in range(SMLP_GROUPS):
            bias = bs_ref[:chunk_len, g * SMLP_GROUP_DIM:(g + 1) * SMLP_GROUP_DIM]
            cs = slice(g * SMLP_GROUP_DIM, (g + 1) * SMLP_GROUP_DIM)
            for c in range(sub // chunk_len):
                rs = slice(c * chunk_len, (c + 1) * chunk_len)
                s = jnp.dot(wgs[g], vb[rs, cs], preferred_element_type=F32) + bias
                yb_ref[i * sub + c * chunk_len:i * sub + (c + 1) * chunk_len, cs] = (u[rs, cs] * s).astype(BF16)


def _inproj(x, ln0_g, ln0_b, wqkvz, wlg, wuv, smlp_g, smlp_b, w_s, bs_full, *, apply_ln0, chunk_len,
            tm, n_sub, want_v):
    n = x.shape[0]
    assert n % tm == 0 and tm % n_sub == 0 and (tm // n_sub) % chunk_len == 0
    row_spec = lambda w: pl.BlockSpec((tm, w), lambda i: (i, 0))
    kern = functools.partial(_inproj_kernel, apply_ln0=apply_ln0, chunk_len=chunk_len, n_sub=n_sub)
    out_specs = [row_spec(QKV_WIDTH), row_spec(GDN_WIDTH), row_spec(LOGIT_PAD), row_spec(SMLP_WIDTH)]
    out_shape = [
        jax.ShapeDtypeStruct((n, QKV_WIDTH), F32),
        jax.ShapeDtypeStruct((n, GDN_WIDTH), F32),
        jax.ShapeDtypeStruct((n, LOGIT_PAD), F32),
        jax.ShapeDtypeStruct((n, SMLP_WIDTH), BF16),
    ]
    if want_v:
        out_specs.append(row_spec(SMLP_WIDTH))
        out_shape.append(jax.ShapeDtypeStruct((n, SMLP_WIDTH), F32))
    return pl.pallas_call(
        kern,
        grid=(n // tm,),
        in_specs=[
            row_spec(D_MODEL),
            _const_spec((1, D_MODEL)), _const_spec((1, D_MODEL)),
            _const_spec((D_MODEL, QKV_WIDTH + GDN_WIDTH)),
            _const_spec((D_MODEL, LOGIT_PAD)),
            _const_spec((D_MODEL, 2 * SMLP_WIDTH)),
            _const_spec((1, SMLP_WIDTH)), _const_spec((1, SMLP_WIDTH)),
            _const_spec((SMLP_GROUPS, SMLP_CHUNK, SMLP_CHUNK)),
            _const_spec((SMLP_CHUNK, SMLP_WIDTH)),
        ],
        out_specs=out_specs,
        out_shape=out_shape,
        compiler_params=pltpu.CompilerParams(dimension_semantics=("arbitrary",),
                                             vmem_limit_bytes=V7X_VMEM_LIMIT_BYTES),
        name="inproj_smlp",
    )(x, ln0_g, ln0_b, wqkvz, wlg, wuv, smlp_g, smlp_b, w_s, bs_full)


def _gdn_kernel(qkv_ref, z_ref, lg_ref, tail0_ref, s0_ref, convw_ref, hp_ref, ng_ref,
                ya_ref, tail_ref, sfin_ref, xc_ref, qkvc_ref, s_ref, *, tb, n_sub):
    t_idx = pl.program_id(1)
    n_t = pl.num_programs(1)

    @pl.when(t_idx == 0)
    def _():
        xc_ref[0:TAIL_ROWS, :] = tail0_ref[0]
        s_ref[...] = s0_ref[0]

    xc_ref[TAIL_ROWS:TAIL_ROWS + tb, :] = qkv_ref[...]
    sub = tb // n_sub
    for i in range(n_sub):
        _gdn_rows(i * sub, sub, xc_ref, convw_ref, qkvc_ref, lg_ref, hp_ref, ng_ref, z_ref, s_ref, ya_ref)
    new_tail = xc_ref[tb:tb + TAIL_ROWS, :]
    xc_ref[0:TAIL_ROWS, :] = new_tail

    @pl.when(t_idx == n_t - 1)
    def _():
        tail_ref[0] = new_tail
        sfin_ref[0] = s_ref[...]


def _gdn_rows(r0, nr, xc_ref, convw_ref, qkvc_ref, lg_ref, hp_ref, ng_ref, z_ref, s_ref, ya_ref):
    C = DELTA_CHUNK
    zrows = xc_ref[r0:r0 + nr + TAIL_ROWS, :]
    acc = zrows * convw_ref[0:1, :]
    for j in range(1, CONV_W):
        acc = zrows * convw_ref[j:j + 1, :] + pltpu.roll(acc, 1, axis=0)
    acc = acc[TAIL_ROWS:, :]
    qkvc_ref[r0:r0 + nr, :] = acc * jax.nn.sigmoid(acc)

    lg = lg_ref[r0:r0 + nr, :]
    beta_all = jax.nn.sigmoid(lg)
    g_all = -jnp.exp(hp_ref[0:1, :]) * jax.nn.softplus(lg + hp_ref[1:2, :])
    rowc = lax.broadcasted_iota(jnp.int32, (nr, LOGIT_PAD), 0) % C
    gcum_all = g_all
    sh = 1
    while sh < C:
        gcum_all = gcum_all + jnp.where(rowc >= sh, pltpu.roll(gcum_all, sh, axis=0), 0.0)
        sh *= 2

    row = lax.broadcasted_iota(jnp.int32, (C, C), 0)
    col = lax.broadcasted_iota(jnp.int32, (C, C), 1)
    causal = col <= row
    strict = col < row
    ng = ng_ref[...]

    eye = jnp.where(row == col, 1.0, 0.0).astype(F32)
    n_chunks = nr // C
    items = [(c, h) for c in range(n_chunks) for h in range(GDN_HEADS)]

    st = {}
    for c in range(n_chunks):
        rs = slice(c * C, (c + 1) * C)
        rr = slice(r0 + c * C, r0 + (c + 1) * C)
        gcum_c = gcum_all[rs, :]
        gcum_t = gcum_c.T
        beta_c = beta_all[rs, :]
        for h in range(GDN_HEADS):
            q = qkvc_ref[rr, h * HEAD_DIM:(h + 1) * HEAD_DIM]
            k = qkvc_ref[rr, GDN_WIDTH + h * HEAD_DIM:GDN_WIDTH + (h + 1) * HEAD_DIM]
            v = qkvc_ref[rr, 2 * GDN_WIDTH + h * HEAD_DIM:2 * GDN_WIDTH + (h + 1) * HEAD_DIM]
            q = q * lax.rsqrt(jnp.sum(q * q, axis=-1, keepdims=True) + NORM_EPS) * (HEAD_DIM ** -0.5)
            k = k * lax.rsqrt(jnp.sum(k * k, axis=-1, keepdims=True) + NORM_EPS)
            beta = beta_c[:, h:h + 1]
            gc = gcum_c[:, GDN_HEADS + h:GDN_HEADS + h + 1]
            grow = gcum_t[GDN_HEADS + h:GDN_HEADS + h + 1, :]
            decay = jnp.exp(jnp.where(causal, gc - grow, -jnp.inf))
            kb = k * beta
            eg = jnp.exp(gc)
            g_last = gc[C - 1:C, :]
            st[c, h] = dict(
                q=q, k=k, kb=kb, decay=decay,
                rhs=jnp.concatenate([v * beta, kb * eg], axis=-1),
                k_dec_t=(k * jnp.exp(g_last - gc)).T,
                q_dec=q * eg, s_scale=jnp.exp(g_last))

    for it in items:
        d = st[it]
        both = _dot_nt(jnp.concatenate([d["kb"], d["q"]], axis=0), d["k"])
        d["p"] = jnp.where(strict, both[:C] * d["decay"], 0.0)
        d["attn"] = both[C:] * d["decay"]
        d["t"] = eye - d["p"]
    span = 1
    while span < C:
        last = 2 * span >= C
        for it in items:
            d = st[it]
            if span == 1:
                d["p"] = _dot(d["p"], d["p"])
            elif last:
                d["t"] = d["t"] + _dot(d["t"], d["p"])
            else:
                both = _dot(jnp.concatenate([d["p"], d["t"]], axis=0), d["p"])
                d["t"] = d["t"] + both[C:]
                d["p"] = both[:C]
        span *= 2
    for it in items:
        d = st[it]
        sol = _dot(d["t"], d["rhs"])
        d["u"] = sol[:, :HEAD_DIM]
        d["w"] = sol[:, HEAD_DIM:]

    for c in range(n_chunks):
        rr = slice(r0 + c * C, r0 + (c + 1) * C)
        s_old = [s_ref[h] for h in range(GDN_HEADS)]
        r = [_dot(jnp.concatenate([st[c, h]["w"], st[c, h]["q_dec"]], axis=0), s_old[h])
             for h in range(GDN_HEADS)]
        v_new = [st[c, h]["u"] - r[h][:C] for h in range(GDN_HEADS)]
        ds = [_dot(st[c, h]["k_dec_t"], v_new[h]) for h in range(GDN_HEADS)]
        oa = [_dot(st[c, h]["attn"], v_new[h]) for h in range(GDN_HEADS)]
        for h in range(GDN_HEADS):
            hs = slice(h * HEAD_DIM, (h + 1) * HEAD_DIM)
            s_ref[h] = s_old[h] * st[c, h]["s_scale"] + ds[h]
            o = r[h][C:] + oa[h]
            o = o * lax.rsqrt(jnp.mean(o * o, axis=-1, keepdims=True) + NORM_EPS)
            zc = z_ref[rr, hs]
            o = o * ng * (zc * jax.nn.sigmoid(zc))
            ya_ref[rr, hs] = o.astype(BF16)


def _gdn(qkv, z, lg, tail0, s0, convw, hp, ng, *, bsz, t, tb):
    n_sub = GDN_ROW_GROUPS if tb % (GDN_ROW_GROUPS * DELTA_CHUNK) == 0 else 1
    assert t % tb == 0 and tb % (n_sub * DELTA_CHUNK) == 0
    nt = t // tb
    row_spec = lambda w: pl.BlockSpec((tb, w), lambda b, i: (b * nt + i, 0))
    kern = functools.partial(_gdn_kernel, tb=tb, n_sub=n_sub)
    return pl.pallas_call(
        kern,
        grid=(bsz, nt),
        in_specs=[
            row_spec(QKV_WIDTH), row_spec(GDN_WIDTH), row_spec(LOGIT_PAD),
            pl.BlockSpec((1, TAIL_ROWS, QKV_WIDTH), lambda b, i: (b, 0, 0)),
            pl.BlockSpec((1, GDN_HEADS, HEAD_DIM, HEAD_DIM), lambda b, i: (b, 0, 0, 0)),
            _const_spec((TAIL_ROWS, QKV_WIDTH)),
            _const_spec((TAIL_ROWS, LOGIT_PAD)),
            _const_spec((1, HEAD_DIM)),
        ],
        out_specs=[
            row_spec(GDN_WIDTH),
            pl.BlockSpec((1, TAIL_ROWS, QKV_WIDTH), lambda b, i: (b, 0, 0)),
            pl.BlockSpec((1, GDN_HEADS, HEAD_DIM, HEAD_DIM), lambda b, i: (b, 0, 0, 0)),
        ],
        out_shape=[
            jax.ShapeDtypeStruct((bsz * t, GDN_WIDTH), BF16),
            jax.ShapeDtypeStruct((bsz, TAIL_ROWS, QKV_WIDTH), F32),
            jax.ShapeDtypeStruct((bsz, GDN_HEADS, HEAD_DIM, HEAD_DIM), F32),
        ],
        scratch_shapes=[
            pltpu.VMEM((tb + TAIL_ROWS, QKV_WIDTH), F32),
            pltpu.VMEM((tb, QKV_WIDTH), F32),
            pltpu.VMEM((GDN_HEADS, HEAD_DIM, HEAD_DIM), F32),
        ],
        compiler_params=pltpu.CompilerParams(dimension_semantics=("arbitrary", "arbitrary"),
                                             vmem_limit_bytes=V7X_VMEM_LIMIT_BYTES),
        name="gdn",
    )(qkv, z, lg, tail0, s0, convw, hp, ng)


def _out_ffn_kernel(x_ref, ya_ref, yb_ref, ln0g_ref, ln0b_ref, woa_ref, wob_ref, ln1g_ref, ln1b_ref,
                    wup_ref, wdn_ref, ln2g_ref, ln2b_ref, o_ref, *, apply_ln0, n_sub):
    sub = x_ref.shape[0] // n_sub
    groups = [slice(i * sub, (i + 1) * sub) for i in range(n_sub)]
    mix = [jnp.dot(ya_ref[rs, :], woa_ref[...], preferred_element_type=F32)
           + jnp.dot(yb_ref[rs, :], wob_ref[...], preferred_element_type=F32) for rs in groups]
    x1, hid = [], []
    for i, rs in enumerate(groups):
        x = x_ref[rs, :]
        if apply_ln0:
            x = _layer_norm(x, ln0g_ref[...], ln0b_ref[...])
        x1.append(_layer_norm(DN_ALPHA * x + mix[i], ln1g_ref[...], ln1b_ref[...]))
        hid.append(jnp.dot(x1[i].astype(BF16), wup_ref[...], preferred_element_type=F32))
    ffn = [jnp.dot(jnp.square(jnp.maximum(h, 0.0)).astype(BF16), wdn_ref[...], preferred_element_type=F32)
           for h in hid]
    for i, rs in enumerate(groups):
        o_ref[rs, :] = _layer_norm(DN_ALPHA * x1[i] + ffn[i], ln2g_ref[...], ln2b_ref[...])


def _out_ffn(x, ya, yb, ln0_g, ln0_b, woa, wob, ln1_g, ln1_b, wup, wdn, ln2_g, ln2_b, *, apply_ln0, tm):
    n = x.shape[0]
    assert n % tm == 0 and tm % FFN_ROW_GROUPS == 0
    row_spec = lambda w: pl.BlockSpec((tm, w), lambda i: (i, 0))
    vec = _const_spec((1, D_MODEL))
    kern = functools.partial(_out_ffn_kernel, apply_ln0=apply_ln0, n_sub=FFN_ROW_GROUPS)
    return pl.pallas_call(
        kern,
        grid=(n // tm,),
        in_specs=[
            row_spec(D_MODEL), row_spec(GDN_WIDTH), row_spec(SMLP_WIDTH),
            vec, vec,
            _const_spec((GDN_WIDTH, D_MODEL)), _const_spec((SMLP_WIDTH, D_MODEL)),
            vec, vec,
            _const_spec((D_MODEL, D_FF)), _const_spec((D_FF, D_MODEL)),
            vec, vec,
        ],
        out_specs=row_spec(D_MODEL),
        out_shape=jax.ShapeDtypeStruct((n, D_MODEL), F32),
        compiler_params=pltpu.CompilerParams(dimension_semantics=("arbitrary",),
                                             vmem_limit_bytes=V7X_VMEM_LIMIT_BYTES),
        name="out_ffn",
    )(x, ya, yb, ln0_g, ln0_b, woa, wob, ln1_g, ln1_b, wup, wdn, ln2_g, ln2_b)


def _prep_layer(l, w_in, conv_w, a_log, dt_bias, gdn_norm_g, smlp_ln_g, smlp_ln_b, w_s, b_s, w_out,
                ln1_g, ln1_b, w_up, w_down, ln2_g, ln2_b):
    wi = w_in[l]
    z_end = QKV_WIDTH + GDN_WIDTH
    lg_end = z_end + 2 * GDN_HEADS
    wqkvz = wi[:, :z_end].astype(BF16)
    wlg = jnp.pad(wi[:, z_end:lg_end], ((0, 0), (0, LOGIT_PAD - 2 * GDN_HEADS))).astype(BF16)
    wuv = wi[:, lg_end:].astype(BF16)
    convw = jnp.pad(conv_w[l], ((0, TAIL_ROWS - CONV_W), (0, 0)))
    hp = jnp.zeros((TAIL_ROWS, LOGIT_PAD), F32)
    hp = hp.at[0, GDN_HEADS:2 * GDN_HEADS].set(a_log[l]).at[1, GDN_HEADS:2 * GDN_HEADS].set(dt_bias[l])
    ng = gdn_norm_g[l].reshape(1, HEAD_DIM)
    bs_full = jnp.repeat(jnp.transpose(b_s[l]), SMLP_GROUP_DIM, axis=1)
    return dict(
        wqkvz=wqkvz, wlg=wlg, wuv=wuv, convw=convw, hp=hp, ng=ng,
        smlp_g=smlp_ln_g[l].reshape(1, -1), smlp_b=smlp_ln_b[l].reshape(1, -1),
        w_s=w_s[l], bs_full=bs_full,
        woa=w_out[l, :GDN_WIDTH].astype(BF16), wob=w_out[l, GDN_WIDTH:].astype(BF16),
        ln1_g=ln1_g[l].reshape(1, -1), ln1_b=ln1_b[l].reshape(1, -1),
        wup=w_up[l].astype(BF16), wdn=w_down[l].astype(BF16),
        ln2_g=ln2_g[l].reshape(1, -1), ln2_b=ln2_b[l].reshape(1, -1),
    )


def _trunk_layer(x, tail0, s0, p, ln0_g, ln0_b, *, bsz, t, apply_ln0, tm, tm_in, n_sub, tb, want_v):
    chunk_len = min(SMLP_CHUNK, t)
    outs = _inproj(x, ln0_g, ln0_b, p["wqkvz"], p["wlg"], p["wuv"], p["smlp_g"], p["smlp_b"],
                   p["w_s"], p["bs_full"], apply_ln0=apply_ln0, chunk_len=chunk_len, tm=tm_in,
                   n_sub=n_sub, want_v=want_v)
    qkv, z, lg, yb = outs[:4]
    vrows = outs[4] if want_v else None
    tail_pad = jnp.pad(tail0, ((0, 0), (TAIL_ROWS - (CONV_W - 1), 0), (0, 0)))
    ya, tail, s_fin = _gdn(qkv, z, lg, tail_pad, s0, p["convw"], p["hp"], p["ng"], bsz=bsz, t=t, tb=tb)
    x_next = _out_ffn(x, ya, yb, ln0_g, ln0_b, p["woa"], p["wob"], p["ln1_g"], p["ln1_b"], p["wup"],
                      p["wdn"], p["ln2_g"], p["ln2_b"], apply_ln0=apply_ln0, tm=tm)
    return x_next, tail[:, TAIL_ROWS - (CONV_W - 1):], s_fin, vrows


def kernel(x_prompt, x_sample, cache_conv, state_delta, ln0_g, ln0_b, w_in, conv_w, a_log, dt_bias,
           gdn_norm_g, smlp_ln_g, smlp_ln_b, w_s, b_s, w_out, ln1_g, ln1_b, w_up, w_down, ln2_g, ln2_b):
    bp, tp, d = x_prompt.shape
    bs, ts, _ = x_sample.shape
    depth = w_in.shape[0]
    xp = x_prompt.reshape(bp * tp, d)
    xs = x_sample.reshape(bs * ts, d)
    g0 = ln0_g.reshape(1, d)
    b0 = ln0_b.reshape(1, d)
    tail_zero = jnp.zeros((bp, CONV_W - 1, QKV_WIDTH), F32)
    s_zero = jnp.zeros((bp, GDN_HEADS, HEAD_DIM, HEAD_DIM), F32)
    conv_p, delta_p, conv_s, delta_s, v_s = [], [], [], [], []
    n_p, n_s = bp * tp, bs * ts
    for l in range(depth):
        p = _prep_layer(l, w_in, conv_w, a_log, dt_bias, gdn_norm_g, smlp_ln_g, smlp_ln_b, w_s, b_s,
                        w_out, ln1_g, ln1_b, w_up, w_down, ln2_g, ln2_b)
        xp, tl_p, s_p, _ = _trunk_layer(xp, tail_zero, s_zero, p, g0, b0, bsz=bp, t=tp,
                                        apply_ln0=(l == 0), tm=min(FFN_ROWS, n_p),
                                        tm_in=min(INPROJ_ROWS, n_p), n_sub=INPROJ_ROW_GROUPS,
                                        tb=min(GDN_ROWS, tp), want_v=False)
        xs, tl_s, s_s, vr = _trunk_layer(xs, cache_conv[l], state_delta[l], p, g0, b0, bsz=bs, t=ts,
                                         apply_ln0=(l == 0), tm=min(FFN_ROWS, n_s),
                                         tm_in=min(INPROJ_ROWS, n_s), n_sub=INPROJ_ROW_GROUPS,
                                         tb=min(GDN_ROWS, ts), want_v=True)
        conv_p.append(tl_p)
        delta_p.append(s_p)
        conv_s.append(tl_s)
        delta_s.append(s_s)
        v_s.append(vr.reshape(bs, ts, SMLP_WIDTH))
    return (xp.reshape(bp, tp, d), xs.reshape(bs, ts, d), jnp.stack(conv_p), jnp.stack(delta_p),
            jnp.stack(conv_s), jnp.stack(delta_s), jnp.stack(v_s))
```
